```python
import math
import jax, jax.numpy as jnp
from jax import lax
import numpy as np

D_MODEL = 2048
BATCH = 4
SEQ = 8192
DEPTH = 1

MEM_LEN = 256
D_MIX = D_MODEL
HEAD_DIM = 64
NSA_WIDTH = D_MIX // 2
CONV_WIDTH = D_MIX - NSA_WIDTH
NSA_HEADS = NSA_WIDTH // HEAD_DIM
NSA_KV_HEADS = 4
NSA_GROUP = NSA_HEADS // NSA_KV_HEADS
CMP_BLOCK = 32
CMP_STRIDE = 16
CMP_HIDDEN = 256
SEL_BLOCK = 64
SEL_TOPK = 16
WIN = 512
Q_BLOCK = 128
CONV_K = 31
REL_BUCKETS = 32
REL_MAX_DIST = 128
X_HEADS = 4
X_HEAD_DIM = D_MODEL // X_HEADS
MOE_GROUPS = 4
MOE_EXPERTS_PER_GROUP = 8
MOE_EXPERTS = MOE_GROUPS * MOE_EXPERTS_PER_GROUP
MOE_TOPK = 2
MOE_HIDDEN = 512
MOE_BLOCK = 128
LN_EPS = 1e-5
DEEPNORM_ALPHA = (2 * DEPTH) ** 0.25
DEEPNORM_BETA = (8 * DEPTH) ** -0.25
IN_SIZES = (NSA_HEADS * HEAD_DIM,) + (NSA_KV_HEADS * HEAD_DIM,) * 6 + (NSA_HEADS * 3, 2 * CONV_WIDTH)
IN_COLS = sum(IN_SIZES)

kernel_name = "hybrid_nsa_conformer_hmoe_deepnorm"


def layer_norm(x, g, b):
    xf = x.astype(jnp.float32)
    mu = jnp.mean(xf, axis=-1, keepdims=True)
    var = jnp.mean(jnp.square(xf - mu), axis=-1, keepdims=True)
    return ((xf - mu) * lax.rsqrt(var + LN_EPS) * g + b).astype(x.dtype)


def masked_softmax(s, mask):
    s = jnp.where(mask, s, -jnp.inf)
    m = jnp.max(s, axis=-1, keepdims=True)
    m = jnp.where(jnp.isfinite(m), m, 0.0)
    p = jnp.exp(s - m)
    return p / jnp.maximum(jnp.sum(p, axis=-1, keepdims=True), 1e-30)


def t5_bucket(dist):
    n = jnp.maximum(dist, 0)
    max_exact = REL_BUCKETS // 2
    nf = jnp.maximum(n, 1).astype(jnp.float32)
    large = max_exact + (jnp.log(nf / max_exact) / math.log(REL_MAX_DIST / max_exact)
                         * (REL_BUCKETS - max_exact)).astype(jnp.int32)
    large = jnp.minimum(large, REL_BUCKETS - 1)
    return jnp.where(n < max_exact, n, large)


def compress(k, pe, w1, w2):
    B, Hk, S, dh = k.shape
    ch = k.reshape(B, Hk, S // CMP_STRIDE, CMP_STRIDE, dh)
    blk = jnp.concatenate([ch[:, :, :-1], ch[:, :, 1:]], axis=3) + pe
    n_cmp = blk.shape[2]
    h = jax.nn.gelu(blk.reshape(B, Hk, n_cmp, CMP_BLOCK * dh) @ w1)
    return h @ w2


def cmp_to_sel_overlap(n_cmp, n_sel):
    cs = np.arange(n_cmp) * CMP_STRIDE
    ce = cs + CMP_BLOCK
    ss = np.arange(n_sel) * SEL_BLOCK
    se = ss + SEL_BLOCK
    ov = np.clip(np.minimum(ce[:, None], se[None, :]) - np.maximum(cs[:, None], ss[None, :]), 0, None)
    return jnp.asarray(ov / CMP_BLOCK, dtype=jnp.float32)


def nsa_attention(q, kc, vc, ks, vs, kw, vw, gates, rel_table):
    B, Hk, G, S, dh = q.shape
    n_q = S // Q_BLOCK
    n_sel = S // SEL_BLOCK
    n_top = min(SEL_TOPK, n_sel)
    n_cmp = kc.shape[2]
    cmp_end = jnp.arange(n_cmp, dtype=jnp.int32) * CMP_STRIDE + CMP_BLOCK - 1
    overlap = cmp_to_sel_overlap(n_cmp, n_sel)
    ks_blocks = ks.reshape(B, Hk, n_sel, SEL_BLOCK, dh)
    vs_blocks = vs.reshape(B, Hk, n_sel, SEL_BLOCK, dh)
    kw_pad = jnp.pad(kw, ((0, 0), (0, 0), (WIN, 0), (0, 0)))
    vw_pad = jnp.pad(vw, ((0, 0), (0, 0), (WIN, 0), (0, 0)))
    tab = rel_table.reshape(REL_BUCKETS, Hk, G).transpose(1, 2, 0)
    h_ix = jnp.arange(Hk)[None, :, None, None, None]
    g_ix = jnp.arange(G)[None, None, :, None, None]
    b_ix = jnp.arange(B)[:, None, None, None]
    hk_ix = jnp.arange(Hk)[None, :, None, None]
    sel_j = jnp.arange(n_sel, dtype=jnp.int32)
    scale = HEAD_DIM ** -0.5

    def block_fn(qi):
        q0 = qi * Q_BLOCK
        qb = lax.dynamic_slice_in_dim(q, q0, Q_BLOCK, axis=3) * scale
        gb = lax.dynamic_slice_in_dim(gates, q0, Q_BLOCK, axis=3)
        t = q0 + jnp.arange(Q_BLOCK, dtype=jnp.int32)

        dist_c = t[:, None] - cmp_end[None, :]
        s_c = jnp.einsum('bhgqd,bhnd->bhgqn', qb, kc, preferred_element_type=jnp.float32)
        s_c = s_c + tab[:, :, t5_bucket(dist_c)]
        p_c = masked_softmax(s_c, dist_c >= 0)
        o_c = jnp.einsum('bhgqn,bhnd->bhgqd', p_c.astype(vc.dtype), vc)

        imp = jnp.einsum('bhgqn,nj->bhqj', p_c, overlap)
        cur = t // SEL_BLOCK
        forced = (sel_j[None, :] == 0) | (sel_j[None, :] == cur[:, None]) | (sel_j[None, :] == cur[:, None] - 1)
        valid = sel_j[None, :] * SEL_BLOCK <= t[:, None]
        score = jnp.where(forced, jnp.inf, jnp.where(valid, imp, -jnp.inf))
        _, idx = lax.top_k(score, n_top)

        k_sel = ks_blocks[b_ix, hk_ix, idx].reshape(B, Hk, Q_BLOCK, n_top * SEL_BLOCK, dh)
        v_sel = vs_blocks[b_ix, hk_ix, idx].reshape(B, Hk, Q_BLOCK, n_top * SEL_BLOCK, dh)
        pos_s = (idx[..., None] * SEL_BLOCK + jnp.arange(SEL_BLOCK, dtype=jnp.int32)).reshape(
            B, Hk, Q_BLOCK, n_top * SEL_BLOCK)
        dist_s = t[:, None] - pos_s
        s_s = jnp.einsum('bhgqd,bhqkd->bhgqk', qb, k_sel, preferred_element_type=jnp.float32)
        s_s = s_s + tab[h_ix, g_ix, t5_bucket(dist_s)[:, :, None]]
        p_s = masked_softmax(s_s, (dist_s >= 0)[:, :, None])
        o_s = jnp.einsum('bhgqk,bhqkd->bhgqd', p_s.astype(v_sel.dtype), v_sel)

        kwb = lax.dynamic_slice_in_dim(kw_pad, q0, Q_BLOCK + WIN, axis=2)
        vwb = lax.dynamic_slice_in_dim(vw_pad, q0, Q_BLOCK + WIN, axis=2)
        pos_w = q0 - WIN + jnp.arange(Q_BLOCK + WIN, dtype=jnp.int32)
        dist_w = t[:, None] - pos_w[None, :]
        mask_w = (dist_w >= 0) & (dist_w < WIN) & (pos_w[None, :] >= 0)
        s_w = jnp.einsum('bhgqd,bhkd->bhgqk', qb, kwb, preferred_element_type=jnp.float32)
        s_w = s_w + tab[:, :, t5_bucket(dist_w)]
        p_w = masked_softmax(s_w, mask_w)
        o_w = jnp.einsum('bhgqk,bhkd->bhgqd', p_w.astype(vwb.dtype), vwb)

        return gb[..., 0:1] * o_c + gb[..., 1:2] * o_s + gb[..., 2:3] * o_w

    out = lax.map(block_fn, jnp.arange(n_q, dtype=jnp.int32))
    return out.transpose(1, 0, 4, 2, 3, 5).reshape(B, S, Hk * G * dh)


def conformer_conv(u, dw_w, dw_b, ln_g, ln_b):
    a, gate = jnp.split(u, 2, axis=-1)
    h = a * jax.nn.sigmoid(gate)
    h = lax.conv_general_dilated(h, dw_w, window_strides=(1,), padding=[(CONV_K - 1, 0)],
                                 dimension_numbers=('NWC', 'WIO', 'NWC'),
                                 feature_group_count=CONV_WIDTH) + dw_b
    return jax.nn.silu(layer_norm(h, ln_g, ln_b))


def memory_cross_attn(x, mem, mem_g, mem_b, wq, wkv, wo):
    B, S, D = x.shape
    m = layer_norm(mem, mem_g, mem_b)
    q = (x @ wq).reshape(B, S, X_HEADS, X_HEAD_DIM)
    k, v = jnp.split(m @ wkv, 2, axis=-1)
    k = k.reshape(B, -1, X_HEADS, X_HEAD_DIM)
    v = v.reshape(B, -1, X_HEADS, X_HEAD_DIM)
    s = jnp.einsum('bshd,bmhd->bhsm', q, k, preferred_element_type=jnp.float32) * (X_HEAD_DIM ** -0.5)
    p = jax.nn.softmax(s, axis=-1)
    o = jnp.einsum('bhsm,bmhd->bshd', p.astype(v.dtype), v).reshape(B, S, X_HEADS * X_HEAD_DIM)
    return o @ wo


def hier_moe(x2d, rg_w, rg_b, re_w, re_b, w_gate, w_up, w_down):
    T, D = x2d.shape
    xf = x2d.astype(jnp.float32)
    lg = xf @ rg_w.astype(jnp.float32) + rg_b.astype(jnp.float32)
    grp = jnp.argmax(lg, axis=-1).astype(jnp.int32)
    w_grp = jnp.take_along_axis(jax.nn.softmax(lg, axis=-1), grp[:, None], axis=-1)[:, 0]
    le = (xf @ re_w.astype(jnp.float32) + re_b.astype(jnp.float32)).reshape(T, MOE_GROUPS, MOE_EXPERTS_PER_GROUP)
    le_g = jnp.take_along_axis(le, grp[:, None, None], axis=1)[:, 0]
    top_v, top_i = lax.top_k(le_g, MOE_TOPK)
    w_tok = w_grp[:, None] * jax.nn.softmax(top_v, axis=-1)
    expert = grp[:, None] * MOE_EXPERTS_PER_GROUP + top_i

    A = T * MOE_TOPK
    e_flat = expert.reshape(A)
    tok_flat = jnp.repeat(jnp.arange(T, dtype=jnp.int32), MOE_TOPK)
    w_flat = w_tok.reshape(A)
    order = jnp.argsort(e_flat)
    e_s, tok_s, w_s = e_flat[order], tok_flat[order], w_flat[order]
    counts = jnp.zeros((MOE_EXPERTS,), jnp.int32).at[e_flat].add(1)
    starts = jnp.cumsum(counts) - counts
    padded = (counts + MOE_BLOCK - 1) // MOE_BLOCK * MOE_BLOCK
    pends = jnp.cumsum(padded)
    pstarts = pends - padded
    dest = pstarts[e_s] + (jnp.arange(A, dtype=jnp.int32) - starts[e_s])
    P = A + MOE_EXPERTS * MOE_BLOCK
    n_blk = P // MOE_BLOCK
    tok_buf = jnp.full((P,), T, jnp.int32).at[dest].set(tok_s)
    w_buf = jnp.zeros((P,), x2d.dtype).at[dest].set(w_s.astype(x2d.dtype))
    blk_e = jnp.minimum(jnp.searchsorted(pends, jnp.arange(n_blk, dtype=jnp.int32) * MOE_BLOCK, side='right'),
                        MOE_EXPERTS - 1).astype(jnp.int32)
    x_pad = jnp.concatenate([x2d, jnp.zeros((1, D), x2d.dtype)], axis=0)

    def run_block(args):
        tok, wb, e = args
        xb = x_pad[tok]
        h = jax.nn.silu(xb @ w_gate[e]) * (xb @ w_up[e])
        return (h @ w_down[e]) * wb[:, None]

    y = lax.map(run_block, (tok_buf.reshape(n_blk, MOE_BLOCK), w_buf.reshape(n_blk, MOE_BLOCK), blk_e))
    out = jnp.zeros((T + 1, D), x2d.dtype).at[tok_buf].add(y.reshape(P, D))
    return out[:T]


def hybrid_layer(x, mem, w_in, cmp_pe_k, cmp_w1_k, cmp_w2_k, cmp_pe_v, cmp_w1_v, cmp_w2_v, rel_table,
                 conv_dw_w, conv_dw_b, conv_ln_g, conv_ln_b, w_out, ln1_g, ln1_b,
                 mem_ln_g, mem_ln_b, xa_wq, xa_wkv, xa_wo, ln2_g, ln2_b,
                 rg_w, rg_b, re_w, re_b, w_gate, w_up, w_down, ln3_g, ln3_b):
    B, S, D = x.shape
    proj = x @ w_in
    cuts = [int(c) for c in np.cumsum(IN_SIZES)[:-1]]
    q, k_c, v_c, k_s, v_s, k_w, v_w, g, u = jnp.split(proj, cuts, axis=-1)

    def kv_heads(t):
        return t.reshape(B, S, NSA_KV_HEADS, HEAD_DIM).transpose(0, 2, 1, 3)

    q = q.reshape(B, S, NSA_KV_HEADS, NSA_GROUP, HEAD_DIM).transpose(0, 2, 3, 1, 4)
    gates = jax.nn.sigmoid(g).reshape(B, S, NSA_KV_HEADS, NSA_GROUP, 3).transpose(0, 2, 3, 1, 4)
    kc = compress(kv_heads(k_c), cmp_pe_k, cmp_w1_k, cmp_w2_k)
    vc = compress(kv_heads(v_c), cmp_pe_v, cmp_w1_v, cmp_w2_v)
    o_nsa = nsa_attention(q, kc, vc, kv_heads(k_s), kv_heads(v_s), kv_heads(k_w), kv_heads(v_w), gates, rel_table)
    o_conv = conformer_conv(u, conv_dw_w, conv_dw_b, conv_ln_g, conv_ln_b)
    mix = jnp.concatenate([o_nsa, o_conv], axis=-1) @ w_out
    x = layer_norm(DEEPNORM_ALPHA * x + mix, ln1_g, ln1_b)

    xa = memory_cross_attn(x, mem, mem_ln_g, mem_ln_b, xa_wq, xa_wkv, xa_wo)
    x = layer_norm(DEEPNORM_ALPHA * x + xa, ln2_g, ln2_b)

    moe = hier_moe(x.reshape(B * S, D), rg_w, rg_b, re_w, re_b, w_gate, w_up, w_down).reshape(B, S, D)
    return layer_norm(DEEPNORM_ALPHA * x + moe, ln3_g, ln3_b)


def setup_inputs(seed: int = 0) -> dict:
    key = jax.random.key(seed)
    keys = iter(jax.random.split(key, 48))

    def nrm(shape, scale):
        return jax.random.normal(next(keys), shape, jnp.float32) * scale

    def gain(shape):
        return 1.0 + nrm(shape, 0.02)

    L = DEPTH
    D = D_MODEL
    return {
        "x": nrm((BATCH, SEQ, D), 1.0),
        "mem": nrm((BATCH, MEM_LEN, D), 1.0),
        "ln_in_g": gain((D,)),
        "ln_in_b": nrm((D,), 0.02),
        "w_in": nrm((L, D, IN_COLS), D ** -0.5),
        "cmp_pe_k": nrm((L, CMP_BLOCK, HEAD_DIM), 0.1),
        "cmp_w1_k": nrm((L, CMP_BLOCK * HEAD_DIM, CMP_HIDDEN), (CMP_BLOCK * HEAD_DIM) ** -0.5),
        "cmp_w2_k": nrm((L, CMP_HIDDEN, HEAD_DIM), CMP_HIDDEN ** -0.5),
        "cmp_pe_v": nrm((L, CMP_BLOCK, HEAD_DIM), 0.1),
        "cmp_w1_v": nrm((L, CMP_BLOCK * HEAD_DIM, CMP_HIDDEN), (CMP_BLOCK * HEAD_DIM) ** -0.5),
        "cmp_w2_v": nrm((L, CMP_HIDDEN, HEAD_DIM), CMP_HIDDEN ** -0.5),
        "rel_table": nrm((REL_BUCKETS, NSA_HEADS), 0.2),
        "conv_dw_w": nrm((L, CONV_K, 1, CONV_WIDTH), CONV_K ** -0.5),
        "conv_dw_b": nrm((L, CONV_WIDTH), 0.02),
        "conv_ln_g": gain((L, CONV_WIDTH)),
        "conv_ln_b": nrm((L, CONV_WIDTH), 0.02),
        "w_out": nrm((L, D_MIX, D), D_MIX ** -0.5 * DEEPNORM_BETA),
        "ln1_g": gain((L, D)),
        "ln1_b": nrm((L, D), 0.02),
        "mem_ln_g": gain((L, D)),
        "mem_ln_b": nrm((L, D), 0.02),
        "xa_wq": nrm((L, D, X_HEADS * X_HEAD_DIM), D ** -0.5),
        "xa_wkv": nrm((L, D, 2 * X_HEADS * X_HEAD_DIM), D ** -0.5),
        "xa_wo": nrm((L, X_HEADS * X_HEAD_DIM, D), (X_HEADS * X_HEAD_DIM) ** -0.5 * DEEPNORM_BETA),
        "ln2_g": gain((L, D)),
        "ln2_b": nrm((L, D), 0.02),
        "router_group_w": nrm((L, D, MOE_GROUPS), D ** -0.5),
        "router_group_b": nrm((L, MOE_GROUPS), 0.01),
        "router_expert_w": nrm((L, D, MOE_EXPERTS), D ** -0.5),
        "router_expert_b": nrm((L, MOE_EXPERTS), 0.01),
        "moe_w_gate": nrm((L, MOE_EXPERTS, D, MOE_HIDDEN), D ** -0.5),
        "moe_w_up": nrm((L, MOE_EXPERTS, D, MOE_HIDDEN), D ** -0.5),
        "moe_w_down": nrm((L, MOE_EXPERTS, MOE_HIDDEN, D), MOE_HIDDEN ** -0.5 * DEEPNORM_BETA),
        "ln3_g": gain((L, D)),
        "ln3_b": nrm((L, D), 0.02),
    }


def reference(x, mem, ln_in_g, ln_in_b, w_in, cmp_pe_k, cmp_w1_k, cmp_w2_k, cmp_pe_v, cmp_w1_v, cmp_w2_v,
              rel_table, conv_dw_w, conv_dw_b, conv_ln_g, conv_ln_b, w_out, ln1_g, ln1_b,
              mem_ln_g, mem_ln_b, xa_wq, xa_wkv, xa_wo, ln2_g, ln2_b,
              router_group_w, router_group_b, router_expert_w, router_expert_b,
              moe_w_gate, moe_w_up, moe_w_down, ln3_g, ln3_b):
    h = layer_norm(x, ln_in_g, ln_in_b)
    for l in range(DEPTH):
        h = hybrid_layer(h, mem, w_in[l], cmp_pe_k[l], cmp_w1_k[l], cmp_w2_k[l], cmp_pe_v[l], cmp_w1_v[l],
                         cmp_w2_v[l], rel_table, conv_dw_w[l], conv_dw_b[l], conv_ln_g[l], conv_ln_b[l],
                         w_out[l], ln1_g[l], ln1_b[l], mem_ln_g[l], mem_ln_b[l], xa_wq[l], xa_wkv[l], xa_wo[l],
                         ln2_g[l], ln2_b[l], router_group_w[l], router_group_b[l], router_expert_w[l],
                         router_expert_b[l], moe_w_gate[l], moe_w_up[l], moe_w_down[l], ln3_g[l], ln3_b[l])
    return h
```

```python
import functools
import math

import jax
import jax.numpy as jnp
import numpy as np
from jax import lax
from jax.experimental import pallas as pl
from jax.experimental.pallas import tpu as pltpu

HEAD_DIM = 64
NSA_HEADS = 16
NSA_KV_HEADS = 4
NSA_GROUP = 4
CMP_BLOCK = 32
CMP_STRIDE = 16
CMP_HIDDEN = 256
SEL_BLOCK = 64
SEL_TOPK = 16
WIN = 512
Q_BLOCK = 128
CONV_K = 31
REL_BUCKETS = 32
REL_MAX_DIST = 128
X_HEADS = 4
MOE_GROUPS = 4
MOE_EPG = 8
MOE_EXPERTS = 32
LN_EPS = 1e-5
DEPTH = 1
ALPHA = (2 * DEPTH) ** 0.25

LANES = 128
VMEM_LIMIT = 56 * 1024 * 1024
ROW_TILE = 256
MOE_ROWS = 256

CMP_FRAME = 512
SEL_FRAME = 128
KEY_PAD = 512
FAR_CHUNK = 512
NEG = -float(2 ** 30)
M_FLOOR = -float(2 ** 27)
COL_ONE, COL_PAD, COL_ONE2 = 64, 65, 66

F32 = jnp.float32
BF16 = jnp.bfloat16


def _dot(a, b):
    return jnp.dot(a, b, preferred_element_type=F32)


def _dot_nt(a, b):
    return lax.dot_general(a, b, (((1,), (1,)), ((), ())), preferred_element_type=F32)


def _ln(x, g, b):
    mu = jnp.mean(x, axis=-1, keepdims=True)
    xc = x - mu
    var = jnp.mean(xc * xc, axis=-1, keepdims=True)
    return xc * lax.rsqrt(var + LN_EPS) * g + b


def _params(sem):
    return pltpu.CompilerParams(dimension_semantics=sem, vmem_limit_bytes=VMEM_LIMIT)


def _const_spec(shape):
    nd = len(shape)
    return pl.BlockSpec(shape, lambda *_: (0,) * nd)


def _in_proj_kernel(x_ref, g_ref, b_ref, wq_ref, wkv_ref, wg_ref, wa_ref, wgt_ref,
                    q_ref, kv_ref, gate_ref, glu_ref):
    h = _ln(x_ref[...], g_ref[...], b_ref[...]).astype(BF16)
    q_ref[...] = (_dot(h, wq_ref[...]) * (HEAD_DIM ** -0.5)).astype(BF16)
    kv_ref[...] = _dot(h, wkv_ref[...]).astype(BF16)
    gate_ref[...] = jax.nn.sigmoid(_dot(h, wg_ref[...]))
    glu_ref[...] = _dot(h, wa_ref[...]) * jax.nn.sigmoid(_dot(h, wgt_ref[...]))


def _in_proj(x2d, g, b, wq, wkv, wg, wa, wgt):
    T, D = x2d.shape
    tm = ROW_TILE
    row = lambda n: pl.BlockSpec((tm, n), lambda i: (i, 0))
    return pl.pallas_call(
        _in_proj_kernel,
        grid=(T // tm,),
        in_specs=[row(D), _const_spec((1, D)), _const_spec((1, D)), _const_spec(wq.shape),
                  _const_spec(wkv.shape), _const_spec(wg.shape), _const_spec(wa.shape),
                  _const_spec(wgt.shape)],
        out_specs=[row(wq.shape[1]), row(wkv.shape[1]), row(wg.shape[1]), row(wa.shape[1])],
        out_shape=[jax.ShapeDtypeStruct((T, wq.shape[1]), BF16),
                   jax.ShapeDtypeStruct((T, wkv.shape[1]), BF16),
                   jax.ShapeDtypeStruct((T, wg.shape[1]), F32),
                   jax.ShapeDtypeStruct((T, wa.shape[1]), F32)],
        compiler_params=_params(("parallel",)),
        name="in_proj",
    )(x2d, g, b, wq, wkv, wg, wa, wgt)


def _compress_kernel(ch_ref, pe_ref, w1a_ref, w1b_ref, w1_ref, w2_ref, tail_ref, out_ref, *, n_rows):
    ch = ch_ref[0, 0]
    first = _dot(ch, w1a_ref[...])
    second = _dot(ch, w1b_ref[...])
    second = pltpu.roll(second, n_rows - 1, 0)
    pe_term = _dot(pe_ref[...], w1_ref[...])[0:1, :]
    hid = jax.nn.gelu(first + second + pe_term).astype(BF16)
    res = _dot(hid, w2_ref[...]) + tail_ref[0:1, :]
    out_ref[0, 0, 0:CMP_FRAME, :] = jnp.broadcast_to(tail_ref[1:2, :], (CMP_FRAME, LANES))
    out_ref[0, 0, CMP_FRAME:CMP_FRAME + n_rows, :] = res


def _compress(ch, pe8, w1a, w1b, w1, w2p, tail):
    B, Hk, n_rows, width = ch.shape
    kern = functools.partial(_compress_kernel, n_rows=n_rows)
    return pl.pallas_call(
        kern,
        grid=(B, Hk),
        in_specs=[pl.BlockSpec((1, 1, n_rows, width), lambda b, h: (b, h, 0, 0)),
                  _const_spec(pe8.shape), _const_spec(w1a.shape), _const_spec(w1b.shape),
                  _const_spec(w1.shape), _const_spec(w2p.shape), _const_spec(tail.shape)],
        out_specs=pl.BlockSpec((1, 1, CMP_FRAME + n_rows, LANES), lambda b, h: (b, h, 0, 0)),
        out_shape=jax.ShapeDtypeStruct((B, Hk, CMP_FRAME + n_rows, LANES), F32),
        compiler_params=_params(("parallel", "parallel")),
        name="compress",
    )(ch, pe8, w1a, w1b, w1, w2p, tail)


def _softmax_update(carry, s, v):
    m, acc = carry
    m_new = jnp.maximum(m, jnp.max(s, axis=-1, keepdims=True))
    p = jnp.exp(s - m_new).astype(BF16)
    acc = acc * jnp.exp(m - m_new) + _dot(p, v)
    return m_new, acc


def _normalise(acc):
    return acc / jnp.maximum(acc[:, COL_ONE:COL_ONE + 1], 1e-30)


def _nsa_kernel(q_ref, gate_ref, kc_ref, vc_ref, ks_ref, vs_ref, kw_ref, vw_ref,
                eye_ref, ovt_ref, tb_ref, tc_ref, qx_ref, pick_ref, place_ref, out_ref):
    qi = pl.program_id(2)
    G, QB = NSA_GROUP, Q_BLOCK
    rows = G * QB

    qblk = q_ref[0]
    q_pad = jnp.concatenate([_dot(qblk, pick_ref[g]) for g in range(G)], axis=0)
    q_pad = (q_pad + qx_ref[0]).astype(BF16)
    q0 = qi * QB

    r_win = pl.multiple_of(q0 - WIN + KEY_PAD, LANES)
    kr = lax.broadcasted_iota(jnp.int32, (rows, QB), 1)
    qrow = lax.broadcasted_iota(jnp.int32, (rows, QB), 0) % QB
    edge = jnp.where(kr > qrow, 0.0, NEG)
    tb = tb_ref[0]
    s_w = _dot_nt(q_pad, kw_ref[0, 0, pl.ds(r_win, WIN), :])
    s_w = s_w + jnp.concatenate([edge, jnp.zeros((rows, WIN - 2 * QB), F32), tb[:, 0:QB]], axis=1)
    init_w = (jnp.full((rows, 1), M_FLOOR, F32), jnp.zeros((rows, LANES), F32))
    carry_w = _softmax_update(init_w, s_w, vw_ref[0, 0, pl.ds(r_win, WIN), :])
    r_diag = pl.multiple_of(r_win + WIN, LANES)
    s_d = _dot_nt(q_pad, kw_ref[0, 0, pl.ds(r_diag, QB), :]) + tb[:, QB:2 * QB]
    _, acc_w = _softmax_update(carry_w, s_d, vw_ref[0, 0, pl.ds(r_diag, QB), :])
    o_w = _normalise(acc_w)

    c0 = pl.multiple_of(8 * qi + 8, 8)
    kc = kc_ref[0, 0, pl.ds(c0, CMP_FRAME), :].astype(BF16)
    vc = vc_ref[0, 0, pl.ds(c0, CMP_FRAME), :].astype(BF16)
    s_c = _dot_nt(q_pad, kc)
    s_c = s_c + jnp.concatenate([jnp.zeros((rows, CMP_FRAME - LANES), F32), tc_ref[0]], axis=1)
    m_c = jnp.maximum(jnp.max(s_c, axis=-1, keepdims=True), M_FLOOR)
    e_c = jnp.exp(s_c - m_c)
    p_c = e_c / jnp.maximum(jnp.sum(e_c, axis=-1, keepdims=True), 1e-30)
    o_c = _dot(p_c.astype(BF16), vc)

    p_sum = p_c[0:QB] + p_c[QB:2 * QB] + p_c[2 * QB:3 * QB] + p_c[3 * QB:4 * QB]
    p_hi = p_sum.astype(BF16)
    p_lo = (p_sum - p_hi.astype(F32)).astype(BF16)
    imp_t = _dot_nt(ovt_ref[...], p_hi) + _dot_nt(ovt_ref[...], p_lo)
    jr = lax.broadcasted_iota(jnp.int32, (SEL_FRAME, QB), 0)
    qr = lax.broadcasted_iota(jnp.int32, (SEL_FRAME, QB), 1)
    j_abs = jr + (2 * qi - (SEL_FRAME - 2))
    cur = (SEL_FRAME - 2) + (qr >= SEL_BLOCK).astype(jnp.int32)
    exists = j_abs >= 0
    forced = exists & ((j_abs == 0) | (jr == cur) | (jr == cur - 1))
    valid = exists & (jr <= cur)
    big = 3.0e38
    score = jnp.where(forced, big, jnp.where(valid, imp_t, -big))
    chosen = jnp.zeros((SEL_FRAME, QB), jnp.bool_)
    for _ in range(SEL_TOPK):
        top = jnp.max(score, axis=0, keepdims=True)
        first = jnp.min(jnp.where(score == top, jr, SEL_FRAME), axis=0, keepdims=True)
        hit = jr == first
        chosen = chosen | hit
        score = jnp.where(hit, -big, score)
    neg_t = jnp.where(chosen & valid, 0.0, NEG)
    neg = neg_t.T.astype(BF16)
    q_sel = jnp.concatenate([jnp.concatenate([neg] * G, axis=0), q_pad], axis=1)

    far_end = q0 - QB + KEY_PAD
    n_far = jnp.where(qi >= 1, (qi + 2) // 4, 0)

    def far_body(i, carry):
        r0 = pl.multiple_of(far_end - FAR_CHUNK * (i + 1), LANES)
        e0 = pl.multiple_of(SEL_BLOCK * (SEL_FRAME - 4 - 8 * (i + 1)) + KEY_PAD, SEL_BLOCK)
        keys = jnp.concatenate([eye_ref[pl.ds(e0, FAR_CHUNK), :], ks_ref[0, 0, pl.ds(r0, FAR_CHUNK), :]], axis=1)
        s = _dot_nt(q_sel, keys)
        return _softmax_update(carry, s, vs_ref[0, 0, pl.ds(r0, FAR_CHUNK), :])

    init = (jnp.full((rows, 1), M_FLOOR, F32), jnp.zeros((rows, LANES), F32))
    carry = lax.fori_loop(0, n_far, far_body, init)
    r_near = pl.multiple_of(far_end, LANES)
    e_near = SEL_BLOCK * (SEL_FRAME - 4) + KEY_PAD
    keys = jnp.concatenate([eye_ref[e_near:e_near + 2 * QB, :], ks_ref[0, 0, pl.ds(r_near, 2 * QB), :]], axis=1)
    s_n = _dot_nt(q_sel, keys) + tb
    _, acc_s = _softmax_update(carry, s_n, vs_ref[0, 0, pl.ds(r_near, 2 * QB), :])
    o_s = _normalise(acc_s)

    gates = gate_ref[0]
    def gate_col(br):
        return jnp.concatenate([gates[:, 3 * g + br:3 * g + br + 1] for g in range(G)], axis=0)
    o = (gate_col(0) * o_c + gate_col(1) * o_s + gate_col(2) * o_w).astype(BF16)
    out = _dot(o[0:QB], place_ref[0])
    for g in range(1, G):
        out = out + _dot(o[g * QB:(g + 1) * QB], place_ref[g])
    out_ref[0] = out.astype(BF16)


def _nsa(q, gates, kc, vc, ks, vs, kw, vw, eye, ovt, tb, tc, qx, pick, place):
    B, S, _ = q.shape
    Hk = NSA_KV_HEADS
    nq = S // Q_BLOCK
    per_head = lambda a: pl.BlockSpec((1, 1) + a.shape[2:], lambda b, h, i: (b, h, 0, 0))
    tab = lambda a: pl.BlockSpec((1,) + a.shape[1:], lambda b, h, i: (h, 0, 0))
    qspec = pl.BlockSpec((1, Q_BLOCK, NSA_GROUP * HEAD_DIM), lambda b, h, i: (b, i, h))
    return pl.pallas_call(
        _nsa_kernel,
        grid=(B, Hk, nq),
        in_specs=[qspec, pl.BlockSpec((1, Q_BLOCK, LANES), lambda b, h, i: (b, i, h)),
                  per_head(kc), per_head(vc), per_head(ks), per_head(vs), per_head(kw), per_head(vw),
                  _const_spec(eye.shape), _const_spec(ovt.shape), tab(tb), tab(tc), tab(qx),
                  _const_spec(pick.shape), _const_spec(place.shape)],
        out_specs=qspec,
        out_shape=jax.ShapeDtypeStruct(q.shape, BF16),
        compiler_params=_params(("parallel", "parallel", "arbitrary")),
        name="nsa",
    )(q, gates, kc, vc, ks, vs, kw, vw, eye, ovt, tb, tc, qx, pick, place)


CONV_HALO = 32


def _conv_kernel(cur_ref, prev_ref, w_ref, b_ref, g_ref, beta_ref, out_ref):
    i = pl.program_id(1)
    cur = cur_ref[0]
    rows = cur.shape[0]
    halo = prev_ref[0, rows - CONV_HALO:rows, :]
    halo = jnp.where(i > 0, halo, 0.0)
    ext = jnp.concatenate([halo, cur], axis=0)
    acc = jnp.zeros_like(cur) + b_ref[...]
    off = CONV_HALO - (CONV_K - 1)
    for k in range(CONV_K):
        acc = acc + ext[off + k:off + k + rows, :] * w_ref[k:k + 1, :]
    y = _ln(acc, g_ref[...], beta_ref[...])
    out_ref[0] = (y * jax.nn.sigmoid(y)).astype(BF16)


def _conv(hglu, w, b, g, beta):
    B, S, C = hglu.shape
    rows = ROW_TILE
    return pl.pallas_call(
        _conv_kernel,
        grid=(B, S // rows),
        in_specs=[pl.BlockSpec((1, rows, C), lambda b, i: (b, i, 0)),
                  pl.BlockSpec((1, rows, C), lambda b, i: (b, jnp.maximum(i - 1, 0), 0)),
                  _const_spec(w.shape), _const_spec(b.shape), _const_spec(g.shape), _const_spec(beta.shape)],
        out_specs=pl.BlockSpec((1, rows, C), lambda b, i: (b, i, 0)),
        out_shape=jax.ShapeDtypeStruct((B, S, C), BF16),
        compiler_params=_params(("parallel", "arbitrary")),
        name="conv",
    )(hglu, hglu, w, b, g, beta)


def _out_proj_kernel(nsa_ref, conv_ref, x_ref, gin_ref, bin_ref, wa_ref, wb_ref, g_ref, b_ref, out_ref):
    h = _ln(x_ref[...], gin_ref[...], bin_ref[...])
    mix = _dot(nsa_ref[...], wa_ref[...]) + _dot(conv_ref[...], wb_ref[...])
    out_ref[...] = _ln(ALPHA * h + mix, g_ref[...], b_ref[...])


def _out_proj(o_nsa, o_conv, x2d, gin, bin_, wa, wb, g, b):
    T, D = x2d.shape
    tm = ROW_TILE
    row = lambda n: pl.BlockSpec((tm, n), lambda i: (i, 0))
    vec = _const_spec((1, D))
    return pl.pallas_call(
        _out_proj_kernel,
        grid=(T // tm,),
        in_specs=[row(o_nsa.shape[1]), row(o_conv.shape[1]), row(D), vec, vec,
                  _const_spec(wa.shape), _const_spec(wb.shape), vec, vec],
        out_specs=row(D),
        out_shape=jax.ShapeDtypeStruct((T, D), F32),
        compiler_params=_params(("parallel",)),
        name="out_proj",
    )(o_nsa, o_conv, x2d, gin, bin_, wa, wb, g, b)


def _mem_kv_kernel(m_ref, g_ref, b_ref, w_ref, out_ref):
    out_ref[...] = _dot(_ln(m_ref[...], g_ref[...], b_ref[...]).astype(BF16), w_ref[...]).astype(BF16)


def _mem_kv(mem2d, g, b, wkv):
    R, D = mem2d.shape
    N = wkv.shape[1]
    tn = 1024
    return pl.pallas_call(
        _mem_kv_kernel,
        grid=(R // ROW_TILE, N // tn),
        in_specs=[pl.BlockSpec((ROW_TILE, D), lambda i, j: (i, 0)), _const_spec((1, D)), _const_spec((1, D)),
                  pl.BlockSpec((D, tn), lambda i, j: (0, j))],
        out_specs=pl.BlockSpec((ROW_TILE, tn), lambda i, j: (i, j)),
        out_shape=jax.ShapeDtypeStruct((R, N), BF16),
        compiler_params=_params(("parallel", "parallel")),
        name="mem_kv",
    )(mem2d, g, b, wkv)


def _xattn_kernel(x_ref, kv_ref, wq_ref, wo_ref, g_ref, b_ref, rw_hi_ref, rw_lo_ref, rb_ref,
                  x2_ref, idx_ref, wt_ref):
    D = x_ref.shape[1]
    dh = D // X_HEADS
    x1 = x_ref[...]
    q = _dot(x1.astype(BF16), wq_ref[...]).astype(BF16)
    heads = []
    for h in range(X_HEADS):
        k = kv_ref[0, :, h * dh:(h + 1) * dh]
        v = kv_ref[0, :, D + h * dh:D + (h + 1) * dh]
        s = _dot_nt(q[:, h * dh:(h + 1) * dh], k) * (dh ** -0.5)
        e = jnp.exp(s - jnp.max(s, axis=-1, keepdims=True))
        p = e / jnp.sum(e, axis=-1, keepdims=True)
        heads.append(_dot(p.astype(BF16), v).astype(BF16))
    xa = _dot(jnp.concatenate(heads, axis=1), wo_ref[...])
    x2 = _ln(ALPHA * x1 + xa, g_ref[...], b_ref[...])
    x2_ref[...] = x2

    x_hi = x2.astype(BF16)
    x_lo = (x2 - x_hi.astype(F32)).astype(BF16)
    logit = (_dot(x_hi, rw_hi_ref[...]) + _dot(x_hi, rw_lo_ref[...]) + _dot(x_lo, rw_hi_ref[...])
             + rb_ref[...])
    lane = lax.broadcasted_iota(jnp.int32, logit.shape, 1)
    big = 3.0e38
    lg = jnp.where(lane < MOE_GROUPS, logit, -big)
    lg_max = jnp.max(lg, axis=-1, keepdims=True)
    grp = jnp.min(jnp.where(lg == lg_max, lane, LANES), axis=-1, keepdims=True)
    w_grp = 1.0 / jnp.sum(jnp.exp(lg - lg_max), axis=-1, keepdims=True)
    e_lane = lane - MOE_GROUPS
    in_grp = (e_lane >= grp * MOE_EPG) & (e_lane < (grp + 1) * MOE_EPG)
    le = jnp.where(in_grp, logit, -big)
    v1 = jnp.max(le, axis=-1, keepdims=True)
    i1 = jnp.min(jnp.where(le == v1, lane, LANES), axis=-1, keepdims=True)
    le2 = jnp.where(lane == i1, -big, le)
    v2 = jnp.max(le2, axis=-1, keepdims=True)
    i2 = jnp.min(jnp.where(le2 == v2, lane, LANES), axis=-1, keepdims=True)
    r = jnp.exp(v2 - v1)
    w1 = w_grp / (1.0 + r)
    w2 = w_grp * r / (1.0 + r)
    idx_ref[...] = jnp.where(lane == 0, i1 - MOE_GROUPS, jnp.where(lane == 1, i2 - MOE_GROUPS, 0))
    wt_ref[...] = jnp.where(lane == 0, w1, jnp.where(lane == 1, w2, 0.0))


def _xattn(x1, mem_kv, S, wq, wo, g, b, rw_hi, rw_lo, rb):
    T, D = x1.shape
    tm = ROW_TILE
    row = lambda n: pl.BlockSpec((tm, n), lambda i: (i, 0))
    vec = _const_spec((1, D))
    per_step = S // tm
    return pl.pallas_call(
        _xattn_kernel,
        grid=(T // tm,),
        in_specs=[row(D), pl.BlockSpec((1,) + mem_kv.shape[1:], lambda i: (i // per_step, 0, 0)),
                  _const_spec(wq.shape), _const_spec(wo.shape), vec, vec,
                  _const_spec(rw_hi.shape), _const_spec(rw_lo.shape), _const_spec(rb.shape)],
        out_specs=[row(D), row(LANES), row(LANES)],
        out_shape=[jax.ShapeDtypeStruct((T, D), F32), jax.ShapeDtypeStruct((T, LANES), jnp.int32),
                   jax.ShapeDtypeStruct((T, LANES), F32)],
        compiler_params=_params(("parallel",)),
        name="xattn",
    )(x1, mem_kv, wq, wo, g, b, rw_hi, rw_lo, rb)


def _row_copy(src_hbm, idx_ref, buf, sem, r):
    return pltpu.make_async_copy(src_hbm.at[pl.ds(idx_ref[0, 0, r], 1), :], buf.at[pl.ds(r, 1), :], sem)


def _gather_start(src_hbm, idx_ref, buf, sem, n_rows):
    def body(r, c):
        _row_copy(src_hbm, idx_ref, buf, sem, r).start()
        return c
    lax.fori_loop(0, n_rows, body, 0)


def _gather_wait(src_hbm, idx_ref, buf, sem, n_rows):
    def body(r, c):
        _row_copy(src_hbm, idx_ref, buf, sem, r).wait()
        return c
    lax.fori_loop(0, n_rows, body, 0)


def _pipelined_gather(src_hbm, cur_idx_ref, nxt_idx_ref, buf, sem, n_rows):
    i = pl.program_id(0)
    slot = i % 2

    @pl.when(i == 0)
    def _():
        _gather_start(src_hbm, cur_idx_ref, buf.at[0], sem.at[0], n_rows)

    @pl.when(i + 1 < pl.num_programs(0))
    def _():
        _gather_start(src_hbm, nxt_idx_ref, buf.at[1 - slot], sem.at[1 - slot], n_rows)

    _gather_wait(src_hbm, cur_idx_ref, buf.at[slot], sem.at[slot], n_rows)
    return slot


def _idx_specs(n_rows, n_steps):
    cur = pl.BlockSpec((1, 1, n_rows), lambda i, *_: (i, 0, 0), memory_space=pltpu.SMEM)
    nxt = pl.BlockSpec((1, 1, n_rows), lambda i, *_: (jnp.minimum(i + 1, n_steps - 1), 0, 0),
                       memory_space=pltpu.SMEM)
    return cur, nxt


def _moe_ffn_kernel(blk_e_ref, tok_ref, tok_nxt_ref, x_hbm, wt_ref, wg_ref, wu_ref, wd_ref, y_ref, buf, sem):
    slot = _pipelined_gather(x_hbm, tok_ref, tok_nxt_ref, buf, sem, MOE_ROWS)
    xb = buf[slot].astype(BF16)
    hid = jax.nn.silu(_dot(xb, wg_ref[0])) * _dot(xb, wu_ref[0])
    y_ref[...] = _dot(hid.astype(BF16), wd_ref[0]) * wt_ref[...]


def _moe_ffn(tok_buf, blk_e, x2, w_buf, w_gate, w_up, w_down):
    T, D = x2.shape
    R = MOE_ROWS
    n_blk = tok_buf.shape[0] // R
    Hd = w_gate.shape[2]
    tok3 = tok_buf.reshape(n_blk, 1, R)
    cur, nxt = _idx_specs(R, n_blk)
    grid_spec = pltpu.PrefetchScalarGridSpec(
        num_scalar_prefetch=1,
        grid=(n_blk,),
        in_specs=[cur, nxt, pl.BlockSpec(memory_space=pl.ANY),
                  pl.BlockSpec((R, 1), lambda i, be: (i, 0)),
                  pl.BlockSpec((1, D, Hd), lambda i, be: (be[i], 0, 0)),
                  pl.BlockSpec((1, D, Hd), lambda i, be: (be[i], 0, 0)),
                  pl.BlockSpec((1, Hd, D), lambda i, be: (be[i], 0, 0))],
        out_specs=pl.BlockSpec((R, D), lambda i, be: (i, 0)),
        scratch_shapes=[pltpu.VMEM((2, R, D), F32), pltpu.SemaphoreType.DMA((2,))],
    )
    return pl.pallas_call(
        _moe_ffn_kernel,
        grid_spec=grid_spec,
        out_shape=jax.ShapeDtypeStruct((n_blk * R, D), F32),
        compiler_params=_params(("arbitrary",)),
        name="moe_ffn",
    )(blk_e, tok3, tok3, x2, w_buf, w_gate, w_up, w_down)


def _combine_kernel(pos_ref, pos_nxt_ref, y_hbm, x_ref, g_ref, b_ref, out_ref, buf, sem):
    slot = _pipelined_gather(y_hbm, pos_ref, pos_nxt_ref, buf, sem, 2 * ROW_TILE)
    moe = buf[slot, 0:ROW_TILE, :] + buf[slot, ROW_TILE:2 * ROW_TILE, :]
    out_ref[...] = _ln(ALPHA * x_ref[...] + moe, g_ref[...], b_ref[...])


def _combine(pos_tiles, y, x2, g, b):
    T, D = x2.shape
    tm = ROW_TILE
    n_steps = T // tm
    pos3 = pos_tiles.reshape(n_steps, 1, 2 * tm)
    cur, nxt = _idx_specs(2 * tm, n_steps)
    return pl.pallas_call(
        _combine_kernel,
        grid=(n_steps,),
        in_specs=[cur, nxt, pl.BlockSpec(memory_space=pl.ANY),
                  pl.BlockSpec((tm, D), lambda i: (i, 0)), _const_spec((1, D)), _const_spec((1, D))],
        out_specs=pl.BlockSpec((tm, D), lambda i: (i, 0)),
        out_shape=jax.ShapeDtypeStruct((T, D), F32),
        scratch_shapes=[pltpu.VMEM((2, 2 * tm, D), F32), pltpu.SemaphoreType.DMA((2,))],
        compiler_params=_params(("arbitrary",)),
        name="moe_combine",
    )(pos3, pos3, y, x2, g, b)


def _t5_bucket_table(n_dist):
    n = np.arange(n_dist)
    max_exact = REL_BUCKETS // 2
    nf = np.maximum(n, 1).astype(np.float64)
    large = max_exact + (np.log(nf / max_exact) / math.log(REL_MAX_DIST / max_exact)
                         * (REL_BUCKETS - max_exact)).astype(np.int64)
    large = np.minimum(large, REL_BUCKETS - 1)
    return np.where(n < max_exact, n, large)


def _bias_tiles(rel_table):
    Hk, G, QB = NSA_KV_HEADS, NSA_GROUP, Q_BLOCK
    tab = rel_table.reshape(REL_BUCKETS, Hk, G).transpose(1, 2, 0)
    far = tab[:, :, REL_BUCKETS - 1]
    far_hi = far.astype(BF16).astype(F32)
    far_lo = far - far_hi

    def tile(dist):
        bucket = _t5_bucket_table(int(dist.max()) + 1)[np.maximum(dist, 0)]
        bias = tab[:, :, bucket] - far[:, :, None, None]
        bias = jnp.where(jnp.asarray(dist >= 0), bias, NEG)
        return bias.reshape(Hk, G * QB, dist.shape[1])

    qr = np.arange(QB)[:, None]
    tb = tile(qr + QB - np.arange(2 * QB)[None, :])
    r = np.arange(LANES)[None, :] + (CMP_FRAME - LANES)
    tc = tile(qr - CMP_STRIDE * (r - (CMP_FRAME - 8)) - (CMP_BLOCK - 1))
    lane = np.arange(LANES)[None, None, None, :]
    qx = (jnp.where(lane == COL_ONE, far_hi[:, :, None, None], 0.0)
          + jnp.where(lane == COL_ONE2, far_lo[:, :, None, None], 0.0)
          + jnp.where(lane == COL_PAD, NEG, 0.0))
    qx = jnp.broadcast_to(qx, (Hk, G, QB, LANES)).reshape(Hk, G * QB, LANES)
    return tb.astype(F32), tc.astype(F32), qx.astype(F32)


def _static_tables(S):
    G = NSA_GROUP
    rows = np.arange(SEL_FRAME * SEL_BLOCK)
    eye = (rows[:, None] // SEL_BLOCK == np.arange(SEL_FRAME)[None, :]).astype(np.float32)
    eye = np.concatenate([np.zeros((KEY_PAD, SEL_FRAME), np.float32), eye], axis=0)
    cs = np.arange(CMP_FRAME) * CMP_STRIDE
    ss = np.arange(SEL_FRAME) * SEL_BLOCK
    ov = np.clip(np.minimum(cs[:, None] + CMP_BLOCK, ss[None, :] + SEL_BLOCK)
                 - np.maximum(cs[:, None], ss[None, :]), 0, None) / CMP_BLOCK
    pick = np.zeros((G, G * HEAD_DIM, LANES), np.float32)
    place = np.zeros((G, LANES, G * HEAD_DIM), np.float32)
    for g in range(G):
        for d in range(HEAD_DIM):
            pick[g, g * HEAD_DIM + d, d] = 1.0
            place[g, d, g * HEAD_DIM + d] = 1.0
    return (jnp.asarray(eye, BF16), jnp.asarray(ov.T, BF16), jnp.asarray(pick, BF16), jnp.asarray(place, BF16))


def _key_layout(t, B, S, ones_cols, pad_col):
    Hk = NSA_KV_HEADS
    t = t.reshape(B, S, Hk, HEAD_DIM).transpose(0, 2, 1, 3)
    lane = np.arange(LANES - HEAD_DIM) + HEAD_DIM
    extra = np.isin(lane, ones_cols).astype(np.float32)
    body = jnp.concatenate([t, jnp.broadcast_to(jnp.asarray(extra, t.dtype), (B, Hk, S, LANES - HEAD_DIM))], axis=-1)
    pad_row = np.zeros((LANES,), np.float32)
    if pad_col is not None:
        pad_row[pad_col] = 1.0
    pad = jnp.broadcast_to(jnp.asarray(pad_row, t.dtype), (B, Hk, KEY_PAD, LANES))
    return jnp.concatenate([pad, body], axis=2)


def _dispatch(idx, wts, T):
    R = MOE_ROWS
    A = 2 * T
    e_flat = idx[:, 0:2].reshape(A)
    w_flat = wts[:, 0:2].reshape(A)
    tok_flat = jnp.repeat(jnp.arange(T, dtype=jnp.int32), 2)
    order = jnp.argsort(e_flat, stable=True)
    e_s = e_flat[order]
    counts = jnp.zeros((MOE_EXPERTS,), jnp.int32).at[e_flat].add(1)
    starts = jnp.cumsum(counts) - counts
    padded = (counts + R - 1) // R * R
    pends = jnp.cumsum(padded)
    pstarts = pends - padded
    dest = pstarts[e_s] + (jnp.arange(A, dtype=jnp.int32) - starts[e_s])
    P = A + MOE_EXPERTS * R
    n_blk = P // R
    tok_buf = jnp.zeros((P,), jnp.int32).at[dest].set(tok_flat[order])
    w_buf = jnp.zeros((P,), F32).at[dest].set(w_flat[order])
    blk_e = jnp.minimum(jnp.searchsorted(pends, jnp.arange(n_blk, dtype=jnp.int32) * R, side='right'),
                        MOE_EXPERTS - 1).astype(jnp.int32)
    pos = jnp.zeros((A,), jnp.int32).at[order].set(dest).reshape(T, 2)
    pos_tiles = pos.reshape(T // ROW_TILE, ROW_TILE, 2).transpose(0, 2, 1).reshape(A)
    return tok_buf, w_buf.reshape(P, 1), blk_e, pos_tiles


def _forward(x, mem, ln_in_g, ln_in_b, w_in, cmp_pe_k, cmp_w1_k, cmp_w2_k, cmp_pe_v, cmp_w1_v, cmp_w2_v, rel_table, conv_dw_w, conv_dw_b, conv_ln_g, conv_ln_b, w_out, ln1_g, ln1_b, mem_ln_g, mem_ln_b, xa_wq, xa_wkv, xa_wo, ln2_g, ln2_b, router_group_w, router_group_b, router_expert_w, router_expert_b, moe_w_gate, moe_w_up, moe_w_down, ln3_g, ln3_b):
    B, S, D = x.shape
    T = B * S
    Hk, G, dh = NSA_KV_HEADS, NSA_GROUP, HEAD_DIM
    assert S % ROW_TILE == 0 and S <= SEL_FRAME * SEL_BLOCK and T % ROW_TILE == 0
    vec = lambda v: v.reshape(1, -1).astype(F32)
    x2d = x.reshape(T, D)

    w = w_in[0]
    nq, nkv = NSA_HEADS * dh, Hk * dh
    c_g = nq + 6 * nkv
    c_u = c_g + NSA_HEADS * 3
    cw = (w.shape[1] - c_u) // 2
    wq = w[:, :nq].astype(BF16)
    wkv = w[:, nq:c_g].astype(BF16)
    wg = jnp.pad(w[:, c_g:c_u].reshape(D, Hk, G * 3), ((0, 0), (0, 0), (0, LANES - G * 3)))
    wg = wg.reshape(D, Hk * LANES).astype(BF16)
    wa = w[:, c_u:c_u + cw].astype(BF16)
    wgt = w[:, c_u + cw:].astype(BF16)
    q, kv, gates, hglu = _in_proj(x2d, vec(ln_in_g), vec(ln_in_b), wq, wkv, wg, wa, wgt)

    def compressed(col, pe, w1, w2, ones_cols, pad_col):
        t = kv[:, col * nkv:(col + 1) * nkv].reshape(B, S, Hk, dh).transpose(0, 2, 1, 3)
        ch = t.reshape(B, Hk, S // CMP_STRIDE, CMP_STRIDE * dh)
        half = CMP_STRIDE * dh
        pe8 = jnp.pad(pe.reshape(1, CMP_BLOCK * dh), ((0, 7), (0, 0))).astype(BF16)
        w2p = jnp.pad(w2, ((0, 0), (0, LANES - dh))).astype(BF16)
        lane = np.arange(LANES)
        tail = np.zeros((8, LANES), np.float32)
        tail[0] = np.isin(lane, ones_cols)
        if pad_col is not None:
            tail[1, pad_col] = 1.0
        w1b16 = w1.astype(BF16)
        return _compress(ch, pe8, w1b16[:half], w1b16[half:], w1b16, w2p, jnp.asarray(tail))

    kc = compressed(0, cmp_pe_k[0], cmp_w1_k[0], cmp_w2_k[0], (COL_ONE, COL_ONE2), COL_PAD)
    vc = compressed(1, cmp_pe_v[0], cmp_w1_v[0], cmp_w2_v[0], (), None)

    part = lambda c: kv[:, c * nkv:(c + 1) * nkv]
    ks = _key_layout(part(2), B, S, (COL_ONE, COL_ONE2), COL_PAD)
    vs = _key_layout(part(3), B, S, (COL_ONE,), None)
    kw = _key_layout(part(4), B, S, (COL_ONE, COL_ONE2), COL_PAD)
    vw = _key_layout(part(5), B, S, (COL_ONE,), None)
    eye, ovt, pick, place = _static_tables(S)
    tb, tc, qx = _bias_tiles(rel_table.astype(F32))
    o_nsa = _nsa(q.reshape(B, S, nq), gates.reshape(B, S, Hk * LANES), kc, vc, ks, vs, kw, vw,
                 eye, ovt, tb, tc, qx, pick, place).reshape(T, nq)

    o_conv = _conv(hglu.reshape(B, S, cw), conv_dw_w[0].reshape(CONV_K, cw), vec(conv_dw_b[0]),
                   vec(conv_ln_g[0]), vec(conv_ln_b[0])).reshape(T, cw)

    wo = w_out[0].astype(BF16)
    x1 = _out_proj(o_nsa, o_conv, x2d, vec(ln_in_g), vec(ln_in_b), wo[:nq], wo[nq:], vec(ln1_g[0]), vec(ln1_b[0]))

    mkv = _mem_kv(mem.reshape(-1, D), vec(mem_ln_g[0]), vec(mem_ln_b[0]), xa_wkv[0].astype(BF16))
    mkv = mkv.reshape(B, mem.shape[1], 2 * D)
    rw = jnp.concatenate([router_group_w[0], router_expert_w[0]], axis=1).astype(F32)
    rw = jnp.pad(rw, ((0, 0), (0, LANES - rw.shape[1])))
    rb = jnp.pad(jnp.concatenate([router_group_b[0], router_expert_b[0]]).astype(F32), (0, LANES - MOE_GROUPS - MOE_EXPERTS))
    rw_hi = rw.astype(BF16)
    rw_lo = (rw - rw_hi.astype(F32)).astype(BF16)
    x2, idx, wts = _xattn(x1, mkv, S, xa_wq[0].astype(BF16), xa_wo[0].astype(BF16), vec(ln2_g[0]), vec(ln2_b[0]),
                          rw_hi, rw_lo, rb.reshape(1, LANES))

    tok_buf, w_buf, blk_e, pos_tiles = _dispatch(idx, wts, T)
    y = _moe_ffn(tok_buf, blk_e, x2, w_buf, moe_w_gate[0].astype(BF16), moe_w_up[0].astype(BF16),
                 moe_w_down[0].astype(BF16))
    out = _combine(pos_tiles, y, x2, vec(ln3_g[0]), vec(ln3_b[0]))
    stages = dict(q=q, kc=kc, vc=vc, o_nsa=o_nsa, o_conv=o_conv, x1=x1, x2=x2)
    return out.reshape(B, S, D), stages


def kernel(x, mem, ln_in_g, ln_in_b, w_in, cmp_pe_k, cmp_w1_k, cmp_w2_k, cmp_pe_v, cmp_w1_v, cmp_w2_v, rel_table, conv_dw_w, conv_dw_b, conv_ln_g, conv_ln_b, w_out, ln1_g, ln1_b, mem_ln_g, mem_ln_b, xa_wq, xa_wkv, xa_wo, ln2_g, ln2_b, router_group_w, router_group_b, router_expert_w, router_expert_b, moe_w_gate, moe_w_up, moe_w_down, ln3_g, ln3_b):
    out, _ = _forward(x, mem, ln_in_g, ln_in_b, w_in, cmp_pe_k, cmp_w1_k, cmp_w2_k, cmp_pe_v, cmp_w1_v, cmp_w2_v, rel_table, conv_dw_w, conv_dw_b, conv_ln_g, conv_ln_b, w_out, ln1_g, ln1_b, mem_ln_g, mem_ln_b, xa_wq, xa_wkv, xa_wo, ln2_g, ln2_b, router_group_w, router_group_b, router_expert_w, router_expert_b, moe_w_gate, moe_w_up, moe_w_down, ln3_g, ln3_b)
    return out
```

```python
import functools
import math

import jax
import jax.numpy as jnp
import numpy as np
from jax import lax
from jax.experimental import pallas as pl
from jax.experimental.pallas import tpu as pltpu

HEAD_DIM = 64
NSA_HEADS = 16
NSA_KV_HEADS = 4
NSA_GROUP = 4
CMP_BLOCK = 32
CMP_STRIDE = 16
CMP_HIDDEN = 256
SEL_BLOCK = 64
SEL_TOPK = 16
WIN = 512
Q_BLOCK = 128
CONV_K = 31
REL_BUCKETS = 32
REL_MAX_DIST = 128
X_HEADS = 4
MOE_GROUPS = 4
MOE_EPG = 8
MOE_EXPERTS = 32
LN_EPS = 1e-5
DEPTH = 1
ALPHA = (2 * DEPTH) ** 0.25

LANES = 128
VMEM_LIMIT = 56 * 1024 * 1024
ROW_TILE = 256
MOE_ROWS = 256

CMP_FRAME = 512
SEL_FRAME = 128
KEY_PAD = 512
FAR_CHUNK = 512
NEG = -float(2 ** 30)
M_FLOOR = -float(2 ** 27)
LOG2E = math.log2(math.e)
COL_ONE, COL_PAD, COL_ONE2 = 64, 65, 66

F32 = jnp.float32
BF16 = jnp.bfloat16


def _dot(a, b):
    return jnp.dot(a, b, preferred_element_type=F32)


def _dot_nt(a, b):
    return lax.dot_general(a, b, (((1,), (1,)), ((), ())), preferred_element_type=F32)


def _ln(x, g, b):
    mu = jnp.mean(x, axis=-1, keepdims=True)
    xc = x - mu
    var = jnp.mean(xc * xc, axis=-1, keepdims=True)
    return xc * lax.rsqrt(var + LN_EPS) * g + b


def _params(sem):
    return pltpu.CompilerParams(dimension_semantics=sem, vmem_limit_bytes=VMEM_LIMIT)


def _const_spec(shape):
    nd = len(shape)
    return pl.BlockSpec(shape, lambda *_: (0,) * nd)


def _in_proj_kernel(x_ref, g_ref, b_ref, wq_ref, wkv_ref, wg_ref, wa_ref, wgt_ref,
                    q_ref, kv_ref, gate_ref, glu_ref):
    h = _ln(x_ref[...], g_ref[...], b_ref[...]).astype(BF16)
    q_ref[...] = (_dot(h, wq_ref[...]) * (HEAD_DIM ** -0.5 * LOG2E)).astype(BF16)
    kv_ref[...] = _dot(h, wkv_ref[...]).astype(BF16)
    gate_ref[...] = jax.nn.sigmoid(_dot(h, wg_ref[...]))
    glu_ref[...] = _dot(h, wa_ref[...]) * jax.nn.sigmoid(_dot(h, wgt_ref[...]))


def _in_proj(x2d, g, b, wq, wkv, wg, wa, wgt):
    T, D = x2d.shape
    tm = ROW_TILE
    row = lambda n: pl.BlockSpec((tm, n), lambda i: (i, 0))
    return pl.pallas_call(
        _in_proj_kernel,
        grid=(T // tm,),
        in_specs=[row(D), _const_spec((1, D)), _const_spec((1, D)), _const_spec(wq.shape),
                  _const_spec(wkv.shape), _const_spec(wg.shape), _const_spec(wa.shape),
                  _const_spec(wgt.shape)],
        out_specs=[row(wq.shape[1]), row(wkv.shape[1]), row(wg.shape[1]), row(wa.shape[1])],
        out_shape=[jax.ShapeDtypeStruct((T, wq.shape[1]), BF16),
                   jax.ShapeDtypeStruct((T, wkv.shape[1]), BF16),
                   jax.ShapeDtypeStruct((T, wg.shape[1]), F32),
                   jax.ShapeDtypeStruct((T, wa.shape[1]), F32)],
        compiler_params=_params(("parallel",)),
        name="in_proj",
    )(x2d, g, b, wq, wkv, wg, wa, wgt)


def _compress_kernel(ch_ref, pe_ref, w1a_ref, w1b_ref, w1_ref, w2_ref, tail_ref, out_ref, *, n_rows):
    ch = ch_ref[0, 0]
    first = _dot(ch, w1a_ref[...])
    second = _dot(ch, w1b_ref[...])
    second = pltpu.roll(second, n_rows - 1, 0)
    pe_term = _dot(pe_ref[...], w1_ref[...])[0:1, :]
    hid = jax.nn.gelu(first + second + pe_term).astype(BF16)
    res = _dot(hid, w2_ref[...]) + tail_ref[0:1, :]
    out_ref[0, 0, 0:CMP_FRAME, :] = jnp.broadcast_to(tail_ref[1:2, :], (CMP_FRAME, LANES))
    out_ref[0, 0, CMP_FRAME:CMP_FRAME + n_rows, :] = res


def _compress(ch, pe8, w1a, w1b, w1, w2p, tail):
    B, Hk, n_rows, width = ch.shape
    kern = functools.partial(_compress_kernel, n_rows=n_rows)
    return pl.pallas_call(
        kern,
        grid=(B, Hk),
        in_specs=[pl.BlockSpec((1, 1, n_rows, width), lambda b, h: (b, h, 0, 0)),
                  _const_spec(pe8.shape), _const_spec(w1a.shape), _const_spec(w1b.shape),
                  _const_spec(w1.shape), _const_spec(w2p.shape), _const_spec(tail.shape)],
        out_specs=pl.BlockSpec((1, 1, CMP_FRAME + n_rows, LANES), lambda b, h: (b, h, 0, 0)),
        out_shape=jax.ShapeDtypeStruct((B, Hk, CMP_FRAME + n_rows, LANES), F32),
        compiler_params=_params(("parallel", "parallel")),
        name="compress",
    )(ch, pe8, w1a, w1b, w1, w2p, tail)


def _softmax_update(carry, s, v):
    m, acc = carry
    m_new = jnp.maximum(m, jnp.max(s, axis=-1, keepdims=True))
    p = jnp.exp2(s - m_new).astype(BF16)
    acc = acc * jnp.exp2(m - m_new) + _dot(p, v)
    return m_new, acc


def _softmax_merge(a, b):
    m = jnp.maximum(a[0], b[0])
    return m, a[1] * jnp.exp2(a[0] - m) + b[1] * jnp.exp2(b[0] - m)


def _add_on_lanes(s, lo, tile):
    hi = lo + tile.shape[1]
    parts = [s[:, :lo]] if lo else []
    parts.append(s[:, lo:hi] + tile)
    if hi < s.shape[1]:
        parts.append(s[:, hi:])
    return jnp.concatenate(parts, axis=1)


def _normalise(acc):
    return acc / jnp.maximum(acc[:, COL_ONE:COL_ONE + 1], 1e-30)


def _nsa_kernel(q_ref, gate_ref, kc_ref, vc_ref, ks_ref, vs_ref, kw_ref, vw_ref,
                eye_ref, ovt_ref, tb_ref, tc_ref, qx_ref, pick_ref, place_ref, out_ref):
    qi = pl.program_id(2)
    G, QB = NSA_GROUP, Q_BLOCK
    rows = G * QB

    qblk = q_ref[0]
    q_pad = jnp.concatenate([_dot(qblk, pick_ref[g]) for g in range(G)], axis=0)
    q_pad = (q_pad + qx_ref[0]).astype(BF16)
    q0 = qi * QB

    r_win = pl.multiple_of(q0 - WIN + KEY_PAD, LANES)
    kr = lax.broadcasted_iota(jnp.int32, (rows, QB), 1)
    qrow = lax.broadcasted_iota(jnp.int32, (rows, QB), 0) % QB
    edge = jnp.where(kr > qrow, 0.0, NEG)
    tb = tb_ref[0]
    s_w = _dot_nt(q_pad, kw_ref[0, 0, pl.ds(r_win, WIN), :])
    s_w = _add_on_lanes(_add_on_lanes(s_w, 0, edge), WIN - QB, tb[:, 0:QB])
    init_w = (jnp.full((rows, 1), M_FLOOR, F32), jnp.zeros((rows, LANES), F32))
    carry_w = _softmax_update(init_w, s_w, vw_ref[0, 0, pl.ds(r_win, WIN), :])
    r_diag = pl.multiple_of(r_win + WIN, LANES)
    s_d = _dot_nt(q_pad, kw_ref[0, 0, pl.ds(r_diag, QB), :]) + tb[:, QB:2 * QB]
    _, acc_w = _softmax_update(carry_w, s_d, vw_ref[0, 0, pl.ds(r_diag, QB), :])
    o_w = _normalise(acc_w)

    c0 = pl.multiple_of(8 * qi + 8, 8)
    kc = kc_ref[0, 0, pl.ds(c0, CMP_FRAME), :].astype(BF16)
    vc = vc_ref[0, 0, pl.ds(c0, CMP_FRAME), :].astype(BF16)
    s_c = _add_on_lanes(_dot_nt(q_pad, kc), CMP_FRAME - LANES, tc_ref[0])
    m_c = jnp.maximum(jnp.max(s_c, axis=-1, keepdims=True), M_FLOOR)
    e_c = jnp.exp2(s_c - m_c)
    p_c = e_c * (1.0 / jnp.maximum(jnp.sum(e_c, axis=-1, keepdims=True), 1e-30))
    o_c = _dot(p_c.astype(BF16), vc)

    p_sum = p_c[0:QB] + p_c[QB:2 * QB] + p_c[2 * QB:3 * QB] + p_c[3 * QB:4 * QB]
    p_hi = p_sum.astype(BF16)
    p_lo = (p_sum - p_hi.astype(F32)).astype(BF16)
    imp_t = _dot_nt(ovt_ref[...], p_hi) + _dot_nt(ovt_ref[...], p_lo)
    jr = lax.broadcasted_iota(jnp.int32, (SEL_FRAME, QB), 0)
    qr = lax.broadcasted_iota(jnp.int32, (SEL_FRAME, QB), 1)
    j_abs = jr + (2 * qi - (SEL_FRAME - 2))
    cur = (SEL_FRAME - 2) + (qr >= SEL_BLOCK).astype(jnp.int32)
    exists = j_abs >= 0
    forced = exists & ((j_abs == 0) | (jr == cur) | (jr == cur - 1))
    valid = exists & (jr <= cur)
    free = valid & jnp.logical_not(forced)
    big = 3.0e38
    score = jnp.where(free, imp_t, -big)
    chosen = jnp.zeros((SEL_FRAME, QB), jnp.bool_)
    for _ in range(SEL_TOPK - 3):
        top = jnp.max(score, axis=0, keepdims=True)
        first = jnp.min(jnp.where(score == top, jr, SEL_FRAME), axis=0, keepdims=True)
        hit = jr == first
        chosen = chosen | hit
        score = jnp.where(hit, -big, score)
    neg_t = jnp.where(forced | (chosen & free), 0.0, NEG)
    neg = neg_t.T.astype(BF16)
    q_sel = jnp.concatenate([jnp.concatenate([neg] * G, axis=0), q_pad], axis=1)

    far_end = q0 - QB + KEY_PAD
    n_far = jnp.where(qi >= 1, (qi + 2) // 4, 0)

    def far_chunk(carry, i, live):
        r0 = pl.multiple_of(jnp.where(live, far_end - FAR_CHUNK * (i + 1), 0), LANES)
        e0 = pl.multiple_of(jnp.where(live, SEL_BLOCK * (SEL_FRAME - 4 - 8 * (i + 1)) + KEY_PAD, 0), SEL_BLOCK)
        keys = jnp.concatenate([eye_ref[pl.ds(e0, FAR_CHUNK), :], ks_ref[0, 0, pl.ds(r0, FAR_CHUNK), :]], axis=1)
        s = _dot_nt(q_sel, keys)
        return _softmax_update(carry, s, vs_ref[0, 0, pl.ds(r0, FAR_CHUNK), :])

    def far_pair(i, carry):
        even, odd = carry
        return far_chunk(even, 2 * i, True), far_chunk(odd, 2 * i + 1, 2 * i + 1 < n_far)

    init = (jnp.full((rows, 1), M_FLOOR, F32), jnp.zeros((rows, LANES), F32))
    even, odd = lax.fori_loop(0, (n_far + 1) // 2, far_pair, (init, init))
    carry = _softmax_merge(even, odd)
    r_near = pl.multiple_of(far_end, LANES)
    e_near = SEL_BLOCK * (SEL_FRAME - 4) + KEY_PAD
    keys = jnp.concatenate([eye_ref[e_near:e_near + 2 * QB, :], ks_ref[0, 0, pl.ds(r_near, 2 * QB), :]], axis=1)
    s_n = _dot_nt(q_sel, keys) + tb
    _, acc_s = _softmax_update(carry, s_n, vs_ref[0, 0, pl.ds(r_near, 2 * QB), :])
    o_s = _normalise(acc_s)

    gates = gate_ref[0]
    def gate_col(br):
        return jnp.concatenate([gates[:, 3 * g + br:3 * g + br + 1] for g in range(G)], axis=0)
    o = (gate_col(0) * o_c + gate_col(1) * o_s + gate_col(2) * o_w).astype(BF16)
    out = _dot(o[0:QB], place_ref[0])
    for g in range(1, G):
        out = out + _dot(o[g * QB:(g + 1) * QB], place_ref[g])
    out_ref[0] = out.astype(BF16)


def _nsa(q, gates, kc, vc, ks, vs, kw, vw, eye, ovt, tb, tc, qx, pick, place):
    B, S, _ = q.shape
    Hk = NSA_KV_HEADS
    nq = S // Q_BLOCK
    per_head = lambda a: pl.BlockSpec((1, 1) + a.shape[2:], lambda b, h, i: (b, h, 0, 0))
    tab = lambda a: pl.BlockSpec((1,) + a.shape[1:], lambda b, h, i: (h, 0, 0))
    qspec = pl.BlockSpec((1, Q_BLOCK, NSA_GROUP * HEAD_DIM), lambda b, h, i: (b, i, h))
    return pl.pallas_call(
        _nsa_kernel,
        grid=(B, Hk, nq),
        in_specs=[qspec, pl.BlockSpec((1, Q_BLOCK, LANES), lambda b, h, i: (b, i, h)),
                  per_head(kc), per_head(vc), per_head(ks), per_head(vs), per_head(kw), per_head(vw),
                  _const_spec(eye.shape), _const_spec(ovt.shape), tab(tb), tab(tc), tab(qx),
                  _const_spec(pick.shape), _const_spec(place.shape)],
        out_specs=qspec,
        out_shape=jax.ShapeDtypeStruct(q.shape, BF16),
        compiler_params=_params(("parallel", "parallel", "arbitrary")),
        name="nsa",
    )(q, gates, kc, vc, ks, vs, kw, vw, eye, ovt, tb, tc, qx, pick, place)


CONV_HALO = 32


def _conv_kernel(cur_ref, prev_ref, w_ref, b_ref, g_ref, beta_ref, out_ref):
    i = pl.program_id(1)
    cur = cur_ref[0]
    rows = cur.shape[0]
    halo = prev_ref[0, rows - CONV_HALO:rows, :]
    halo = jnp.where(i > 0, halo, 0.0)
    ext = jnp.concatenate([halo, cur], axis=0)
    acc = jnp.zeros_like(cur) + b_ref[...]
    off = CONV_HALO - (CONV_K - 1)
    for k in range(CONV_K):
        acc = acc + ext[off + k:off + k + rows, :] * w_ref[k:k + 1, :]
    y = _ln(acc, g_ref[...], beta_ref[...])
    out_ref[0] = (y * jax.nn.sigmoid(y)).astype(BF16)


def _conv(hglu, w, b, g, beta):
    B, S, C = hglu.shape
    rows = ROW_TILE
    return pl.pallas_call(
        _conv_kernel,
        grid=(B, S // rows),
        in_specs=[pl.BlockSpec((1, rows, C), lambda b, i: (b, i, 0)),
                  pl.BlockSpec((1, rows, C), lambda b, i: (b, jnp.maximum(i - 1, 0), 0)),
                  _const_spec(w.shape), _const_spec(b.shape), _const_spec(g.shape), _const_spec(beta.shape)],
        out_specs=pl.BlockSpec((1, rows, C), lambda b, i: (b, i, 0)),
        out_shape=jax.ShapeDtypeStruct((B, S, C), BF16),
        compiler_params=_params(("parallel", "arbitrary")),
        name="conv",
    )(hglu, hglu, w, b, g, beta)


def _out_proj_kernel(nsa_ref, conv_ref, x_ref, gin_ref, bin_ref, wa_ref, wb_ref, g_ref, b_ref, out_ref):
    h = _ln(x_ref[...], gin_ref[...], bin_ref[...])
    mix = _dot(nsa_ref[...], wa_ref[...]) + _dot(conv_ref[...], wb_ref[...])
    out_ref[...] = _ln(ALPHA * h + mix, g_ref[...], b_ref[...])


def _out_proj(o_nsa, o_conv, x2d, gin, bin_, wa, wb, g, b):
    T, D = x2d.shape
    tm = ROW_TILE
    row = lambda n: pl.BlockSpec((tm, n), lambda i: (i, 0))
    vec = _const_spec((1, D))
    return pl.pallas_call(
        _out_proj_kernel,
        grid=(T // tm,),
        in_specs=[row(o_nsa.shape[1]), row(o_conv.shape[1]), row(D), vec, vec,
                  _const_spec(wa.shape), _const_spec(wb.shape), vec, vec],
        out_specs=row(D),
        out_shape=jax.ShapeDtypeStruct((T, D), F32),
        compiler_params=_params(("parallel",)),
        name="out_proj",
    )(o_nsa, o_conv, x2d, gin, bin_, wa, wb, g, b)


def _mem_kv_kernel(m_ref, g_ref, b_ref, w_ref, out_ref):
    out_ref[...] = _dot(_ln(m_ref[...], g_ref[...], b_ref[...]).astype(BF16), w_ref[...]).astype(BF16)


def _mem_kv(mem2d, g, b, wkv):
    R, D = mem2d.shape
    N = wkv.shape[1]
    tn = 1024
    return pl.pallas_call(
        _mem_kv_kernel,
        grid=(R // ROW_TILE, N // tn),
        in_specs=[pl.BlockSpec((ROW_TILE, D), lambda i, j: (i, 0)), _const_spec((1, D)), _const_spec((1, D)),
                  pl.BlockSpec((D, tn), lambda i, j: (0, j))],
        out_specs=pl.BlockSpec((ROW_TILE, tn), lambda i, j: (i, j)),
        out_shape=jax.ShapeDtypeStruct((R, N), BF16),
        compiler_params=_params(("parallel", "parallel")),
        name="mem_kv",
    )(mem2d, g, b, wkv)


def _xattn_kernel(x_ref, kv_ref, wq_ref, wo_ref, g_ref, b_ref, rw_hi_ref, rw_lo_ref, rb_ref,
                  x2_ref, idx_ref, wt_ref):
    D = x_ref.shape[1]
    dh = D // X_HEADS
    x1 = x_ref[...]
    q = _dot(x1.astype(BF16), wq_ref[...]).astype(BF16)
    heads = []
    for h in range(X_HEADS):
        k = kv_ref[0, :, h * dh:(h + 1) * dh]
        v = kv_ref[0, :, D + h * dh:D + (h + 1) * dh]
        s = _dot_nt(q[:, h * dh:(h + 1) * dh], k) * (dh ** -0.5)
        e = jnp.exp(s - jnp.max(s, axis=-1, keepdims=True))
        p = e / jnp.sum(e, axis=-1, keepdims=True)
        heads.append(_dot(p.astype(BF16), v).astype(BF16))
    xa = _dot(jnp.concatenate(heads, axis=1), wo_ref[...])
    x2 = _ln(ALPHA * x1 + xa, g_ref[...], b_ref[...])
    x2_ref[...] = x2

    x_hi = x2.astype(BF16)
    x_lo = (x2 - x_hi.astype(F32)).astype(BF16)
    logit = (_dot(x_hi, rw_hi_ref[...]) + _dot(x_hi, rw_lo_ref[...]) + _dot(x_lo, rw_hi_ref[...])
             + rb_ref[...])
    lane = lax.broadcasted_iota(jnp.int32, logit.shape, 1)
    big = 3.0e38
    lg = jnp.where(lane < MOE_GROUPS, logit, -big)
    lg_max = jnp.max(lg, axis=-1, keepdims=True)
    grp = jnp.min(jnp.where(lg == lg_max, lane, LANES), axis=-1, keepdims=True)
    w_grp = 1.0 / jnp.sum(jnp.exp(lg - lg_max), axis=-1, keepdims=True)
    e_lane = lane - MOE_GROUPS
    in_grp = (e_lane >= grp * MOE_EPG) & (e_lane < (grp + 1) * MOE_EPG)
    le = jnp.where(in_grp, logit, -big)
    v1 = jnp.max(le, axis=-1, keepdims=True)
    i1 = jnp.min(jnp.where(le == v1, lane, LANES), axis=-1, keepdims=True)
    le2 = jnp.where(lane == i1, -big, le)
    v2 = jnp.max(le2, axis=-1, keepdims=True)
    i2 = jnp.min(jnp.where(le2 == v2, lane, LANES), axis=-1, keepdims=True)
    r = jnp.exp(v2 - v1)
    w1 = w_grp / (1.0 + r)
    w2 = w_grp * r / (1.0 + r)
    idx_ref[...] = jnp.where(lane == 0, i1 - MOE_GROUPS, jnp.where(lane == 1, i2 - MOE_GROUPS, 0))
    wt_ref[...] = jnp.where(lane == 0, w1, jnp.where(lane == 1, w2, 0.0))


def _xattn(x1, mem_kv, S, wq, wo, g, b, rw_hi, rw_lo, rb):
    T, D = x1.shape
    tm = ROW_TILE
    row = lambda n: pl.BlockSpec((tm, n), lambda i: (i, 0))
    vec = _const_spec((1, D))
    per_step = S // tm
    return pl.pallas_call(
        _xattn_kernel,
        grid=(T // tm,),
        in_specs=[row(D), pl.BlockSpec((1,) + mem_kv.shape[1:], lambda i: (i // per_step, 0, 0)),
                  _const_spec(wq.shape), _const_spec(wo.shape), vec, vec,
                  _const_spec(rw_hi.shape), _const_spec(rw_lo.shape), _const_spec(rb.shape)],
        out_specs=[row(D), row(LANES), row(LANES)],
        out_shape=[jax.ShapeDtypeStruct((T, D), F32), jax.ShapeDtypeStruct((T, LANES), jnp.int32),
                   jax.ShapeDtypeStruct((T, LANES), F32)],
        compiler_params=_params(("parallel",)),
        name="xattn",
    )(x1, mem_kv, wq, wo, g, b, rw_hi, rw_lo, rb)


def _gather_copy(x_hbm, a, xbuf, sem, r):
    tok = jnp.maximum(a, 0) >> 1
    return pltpu.make_async_copy(x_hbm.at[pl.ds(tok, 1), :], xbuf.at[pl.ds(r, 1), :], sem)


def _scatter_copy(ybuf, a, y_hbm, sem, r, n_tok, width, slot):
    real = a >= 0
    row = jnp.where(real, a >> 1, n_tok + r)
    col = pl.multiple_of(jnp.where(real, a & 1, slot) * width, width)
    return pltpu.make_async_copy(ybuf.at[pl.ds(r, 1), :], y_hbm.at[pl.ds(row, 1), pl.ds(col, width)], sem)


def _moe_ffn_kernel(blk_e_ref, prev_ref, cur_ref, nxt_ref, x_hbm, wg_ref, wu_ref, wd_ref, y_hbm,
                    xbuf, ybuf, gsem, ssem):
    i = pl.program_id(0)
    last = pl.num_programs(0) - 1
    slot = i % 2
    other = 1 - slot
    R = MOE_ROWS
    n_tok, width = x_hbm.shape
    rows = range(R)
    gather = lambda ref, s, r: _gather_copy(x_hbm, ref[0, 0, r], xbuf.at[s], gsem.at[s], r)
    scatter = lambda a, s, r: _scatter_copy(ybuf.at[s], a, y_hbm, ssem.at[s], r, n_tok, width, s)

    @pl.when(i == 0)
    def _():
        ybuf[...] = jnp.zeros_like(ybuf)
        for r in rows:
            gather(cur_ref, 0, r).start()
            scatter(-1, 0, r).start()

    gather_wait = lambda s: pltpu.make_async_copy(xbuf.at[s], xbuf.at[s], gsem.at[s]).wait()
    scatter_wait = lambda s: pltpu.make_async_copy(ybuf.at[s], ybuf.at[s], ssem.at[s]).wait()
    gather_wait(slot)
    scatter_wait(slot)
    for r in rows:
        gather(nxt_ref, other, r).start()
        scatter(jnp.where(i > 0, prev_ref[0, 0, r], -1), other, r).start()

    xb = xbuf[slot].astype(BF16)
    hid = jax.nn.silu(_dot(xb, wg_ref[0])) * _dot(xb, wu_ref[0])
    ybuf[slot] = _dot(hid.astype(BF16), wd_ref[0])

    @pl.when(i == last)
    def _():
        for r in rows:
            scatter(cur_ref[0, 0, r], slot, r).start()
        gather_wait(other)
        scatter_wait(other)
        scatter_wait(slot)


def _moe_ffn(a_buf, blk_e, x2, w_gate, w_up, w_down):
    T, D = x2.shape
    R = MOE_ROWS
    n_blk = a_buf.shape[0] // R
    Hd = w_gate.shape[2]
    a3 = a_buf.reshape(n_blk, 1, R)
    idx = lambda f: pl.BlockSpec((1, 1, R), lambda i, be: (f(i), 0, 0), memory_space=pltpu.SMEM)
    grid_spec = pltpu.PrefetchScalarGridSpec(
        num_scalar_prefetch=1,
        grid=(n_blk,),
        in_specs=[idx(lambda i: jnp.maximum(i - 1, 0)), idx(lambda i: i), idx(lambda i: jnp.minimum(i + 1, n_blk - 1)),
                  pl.BlockSpec(memory_space=pl.ANY),
                  pl.BlockSpec((1, D, Hd), lambda i, be: (be[i], 0, 0)),
                  pl.BlockSpec((1, D, Hd), lambda i, be: (be[i], 0, 0)),
                  pl.BlockSpec((1, Hd, D), lambda i, be: (be[i], 0, 0))],
        out_specs=pl.BlockSpec(memory_space=pl.ANY),
        scratch_shapes=[pltpu.VMEM((2, R, D), F32), pltpu.VMEM((2, R, D), F32),
                        pltpu.SemaphoreType.DMA((2,)), pltpu.SemaphoreType.DMA((2,))],
    )
    return pl.pallas_call(
        _moe_ffn_kernel,
        grid_spec=grid_spec,
        out_shape=jax.ShapeDtypeStruct((T + R, 2 * D), F32),
        compiler_params=_params(("arbitrary",)),
        name="moe_ffn",
    )(blk_e, a3, a3, a3, x2, w_gate, w_up, w_down)


def _combine_kernel(y_ref, wt_ref, x_ref, g_ref, b_ref, out_ref):
    D = x_ref.shape[1]
    wt = wt_ref[...]
    moe = wt[:, 0:1] * y_ref[:, 0:D] + wt[:, 1:2] * y_ref[:, D:2 * D]
    out_ref[...] = _ln(ALPHA * x_ref[...] + moe, g_ref[...], b_ref[...])


def _combine(y, wts, x2, g, b):
    T, D = x2.shape
    tm = ROW_TILE
    row = lambda n: pl.BlockSpec((tm, n), lambda i: (i, 0))
    return pl.pallas_call(
        _combine_kernel,
        grid=(T // tm,),
        in_specs=[row(2 * D), row(LANES), row(D), _const_spec((1, D)), _const_spec((1, D))],
        out_specs=row(D),
        out_shape=jax.ShapeDtypeStruct((T, D), F32),
        compiler_params=_params(("parallel",)),
        name="moe_combine",
    )(y, wts, x2, g, b)


def _t5_bucket_table(n_dist):
    n = np.arange(n_dist)
    max_exact = REL_BUCKETS // 2
    nf = np.maximum(n, 1).astype(np.float64)
    large = max_exact + (np.log(nf / max_exact) / math.log(REL_MAX_DIST / max_exact)
                         * (REL_BUCKETS - max_exact)).astype(np.int64)
    large = np.minimum(large, REL_BUCKETS - 1)
    return np.where(n < max_exact, n, large)


def _bias_tiles(rel_table):
    Hk, G, QB = NSA_KV_HEADS, NSA_GROUP, Q_BLOCK
    tab = rel_table.reshape(REL_BUCKETS, Hk, G).transpose(1, 2, 0) * LOG2E
    far = tab[:, :, REL_BUCKETS - 1]
    far_hi = far.astype(BF16).astype(F32)
    far_lo = far - far_hi

    def tile(dist):
        bucket = _t5_bucket_table(int(dist.max()) + 1)[np.maximum(dist, 0)]
        bias = tab[:, :, bucket] - far[:, :, None, None]
        bias = jnp.where(jnp.asarray(dist >= 0), bias, NEG)
        return bias.reshape(Hk, G * QB, dist.shape[1])

    qr = np.arange(QB)[:, None]
    tb = tile(qr + QB - np.arange(2 * QB)[None, :])
    r = np.arange(LANES)[None, :] + (CMP_FRAME - LANES)
    tc = tile(qr - CMP_STRIDE * (r - (CMP_FRAME - 8)) - (CMP_BLOCK - 1))
    lane = np.arange(LANES)[None, None, None, :]
    qx = (jnp.where(lane == COL_ONE, far_hi[:, :, None, None], 0.0)
          + jnp.where(lane == COL_ONE2, far_lo[:, :, None, None], 0.0)
          + jnp.where(lane == COL_PAD, NEG, 0.0))
    qx = jnp.broadcast_to(qx, (Hk, G, QB, LANES)).reshape(Hk, G * QB, LANES)
    return tb.astype(F32), tc.astype(F32), qx.astype(F32)


def _static_tables(S):
    G = NSA_GROUP
    rows = np.arange(SEL_FRAME * SEL_BLOCK)
    eye = (rows[:, None] // SEL_BLOCK == np.arange(SEL_FRAME)[None, :]).astype(np.float32)
    eye = np.concatenate([np.zeros((KEY_PAD, SEL_FRAME), np.float32), eye], axis=0)
    cs = np.arange(CMP_FRAME) * CMP_STRIDE
    ss = np.arange(SEL_FRAME) * SEL_BLOCK
    ov = np.clip(np.minimum(cs[:, None] + CMP_BLOCK, ss[None, :] + SEL_BLOCK)
                 - np.maximum(cs[:, None], ss[None, :]), 0, None) / CMP_BLOCK
    pick = np.zeros((G, G * HEAD_DIM, LANES), np.float32)
    place = np.zeros((G, LANES, G * HEAD_DIM), np.float32)
    for g in range(G):
        for d in range(HEAD_DIM):
            pick[g, g * HEAD_DIM + d, d] = 1.0
            place[g, d, g * HEAD_DIM + d] = 1.0
    return (jnp.asarray(eye, BF16), jnp.asarray(ov.T, BF16), jnp.asarray(pick, BF16), jnp.asarray(place, BF16))


def _key_layout(t, B, S, ones_cols, pad_col):
    Hk = NSA_KV_HEADS
    t = t.reshape(B, S, Hk, HEAD_DIM).transpose(0, 2, 1, 3)
    lane = np.arange(LANES - HEAD_DIM) + HEAD_DIM
    extra = np.isin(lane, ones_cols).astype(np.float32)
    body = jnp.concatenate([t, jnp.broadcast_to(jnp.asarray(extra, t.dtype), (B, Hk, S, LANES - HEAD_DIM))], axis=-1)
    pad_row = np.zeros((LANES,), np.float32)
    if pad_col is not None:
        pad_row[pad_col] = 1.0
    pad = jnp.broadcast_to(jnp.asarray(pad_row, t.dtype), (B, Hk, KEY_PAD, LANES))
    return jnp.concatenate([pad, body], axis=2)


def _dispatch(idx, T):
    R, E = MOE_ROWS, MOE_EXPERTS
    A = 2 * T
    id_bits = max(A - 1, 1).bit_length()
    e_flat = idx[:, 0:2].reshape(A)
    counts = jnp.sum((e_flat[:, None] == jnp.arange(E, dtype=jnp.int32)[None, :]).astype(jnp.int32), axis=0)
    n_pad = (-counts) % R
    real = (e_flat << (id_bits + 1)) | jnp.arange(A, dtype=jnp.int32)
    d = jnp.arange(E * R, dtype=jnp.int32)
    d_e, d_r = d // R, d % R
    pad = jnp.where(d_r < n_pad[d_e], (d_e << (id_bits + 1)) | (1 << id_bits) | d_r, (E << (id_bits + 1)) | d)
    keys = jnp.sort(jnp.concatenate([real, pad]))
    expert = keys >> (id_bits + 1)
    is_pad = ((keys >> id_bits) & 1) == 1
    a_buf = jnp.where(is_pad | (expert >= E), -1, keys & ((1 << id_bits) - 1))
    blk_e = jnp.minimum(expert[::R], E - 1)
    return a_buf, blk_e


def _forward(x, mem, ln_in_g, ln_in_b, w_in, cmp_pe_k, cmp_w1_k, cmp_w2_k, cmp_pe_v, cmp_w1_v, cmp_w2_v, rel_table, conv_dw_w, conv_dw_b, conv_ln_g, conv_ln_b, w_out, ln1_g, ln1_b, mem_ln_g, mem_ln_b, xa_wq, xa_wkv, xa_wo, ln2_g, ln2_b, router_group_w, router_group_b, router_expert_w, router_expert_b, moe_w_gate, moe_w_up, moe_w_down, ln3_g, ln3_b):
    B, S, D = x.shape
    T = B * S
    Hk, G, dh = NSA_KV_HEADS, NSA_GROUP, HEAD_DIM
    assert S % ROW_TILE == 0 and SEL_TOPK * SEL_BLOCK <= S <= SEL_FRAME * SEL_BLOCK
    vec = lambda v: v.reshape(1, -1).astype(F32)
    x2d = x.reshape(T, D)

    w = w_in[0]
    nq, nkv = NSA_HEADS * dh, Hk * dh
    c_g = nq + 6 * nkv
    c_u = c_g + NSA_HEADS * 3
    cw = (w.shape[1] - c_u) // 2
    wq = w[:, :nq].astype(BF16)
    wkv = w[:, nq:c_g].astype(BF16)
    wg = jnp.pad(w[:, c_g:c_u].reshape(D, Hk, G * 3), ((0, 0), (0, 0), (0, LANES - G * 3)))
    wg = wg.reshape(D, Hk * LANES).astype(BF16)
    wa = w[:, c_u:c_u + cw].astype(BF16)
    wgt = w[:, c_u + cw:].astype(BF16)
    q, kv, gates, hglu = _in_proj(x2d, vec(ln_in_g), vec(ln_in_b), wq, wkv, wg, wa, wgt)

    def compressed(col, pe, w1, w2, ones_cols, pad_col):
        t = kv[:, col * nkv:(col + 1) * nkv].reshape(B, S, Hk, dh).transpose(0, 2, 1, 3)
        ch = t.reshape(B, Hk, S // CMP_STRIDE, CMP_STRIDE * dh)
        half = CMP_STRIDE * dh
        pe8 = jnp.pad(pe.reshape(1, CMP_BLOCK * dh), ((0, 7), (0, 0))).astype(BF16)
        w2p = jnp.pad(w2, ((0, 0), (0, LANES - dh))).astype(BF16)
        lane = np.arange(LANES)
        tail = np.zeros((8, LANES), np.float32)
        tail[0] = np.isin(lane, ones_cols)
        if pad_col is not None:
            tail[1, pad_col] = 1.0
        w1b16 = w1.astype(BF16)
        return _compress(ch, pe8, w1b16[:half], w1b16[half:], w1b16, w2p, jnp.asarray(tail))

    kc = compressed(0, cmp_pe_k[0], cmp_w1_k[0], cmp_w2_k[0], (COL_ONE, COL_ONE2), COL_PAD)
    vc = compressed(1, cmp_pe_v[0], cmp_w1_v[0], cmp_w2_v[0], (), None)

    part = lambda c: kv[:, c * nkv:(c + 1) * nkv]
    ks = _key_layout(part(2), B, S, (COL_ONE, COL_ONE2), COL_PAD)
    vs = _key_layout(part(3), B, S, (COL_ONE,), None)
    kw = _key_layout(part(4), B, S, (COL_ONE, COL_ONE2), COL_PAD)
    vw = _key_layout(part(5), B, S, (COL_ONE,), None)
    eye, ovt, pick, place = _static_tables(S)
    tb, tc, qx = _bias_tiles(rel_table.astype(F32))
    o_nsa = _nsa(q.reshape(B, S, nq), gates.reshape(B, S, Hk * LANES), kc, vc, ks, vs, kw, vw,
                 eye, ovt, tb, tc, qx, pick, place).reshape(T, nq)

    o_conv = _conv(hglu.reshape(B, S, cw), conv_dw_w[0].reshape(CONV_K, cw), vec(conv_dw_b[0]),
                   vec(conv_ln_g[0]), vec(conv_ln_b[0])).reshape(T, cw)

    wo = w_out[0].astype(BF16)
    x1 = _out_proj(o_nsa, o_conv, x2d, vec(ln_in_g), vec(ln_in_b), wo[:nq], wo[nq:], vec(ln1_g[0]), vec(ln1_b[0]))

    mkv = _mem_kv(mem.reshape(-1, D), vec(mem_ln_g[0]), vec(mem_ln_b[0]), xa_wkv[0].astype(BF16))
    mkv = mkv.reshape(B, mem.shape[1], 2 * D)
    rw = jnp.concatenate([router_group_w[0], router_expert_w[0]], axis=1).astype(F32)
    rw = jnp.pad(rw, ((0, 0), (0, LANES - rw.shape[1])))
    rb = jnp.pad(jnp.concatenate([router_group_b[0], router_expert_b[0]]).astype(F32), (0, LANES - MOE_GROUPS - MOE_EXPERTS))
    rw_hi = rw.astype(BF16)
    rw_lo = (rw - rw_hi.astype(F32)).astype(BF16)
    x2, idx, wts = _xattn(x1, mkv, S, xa_wq[0].astype(BF16), xa_wo[0].astype(BF16), vec(ln2_g[0]), vec(ln2_b[0]),
                          rw_hi, rw_lo, rb.reshape(1, LANES))

    a_buf, blk_e = _dispatch(idx, T)
    y = _moe_ffn(a_buf, blk_e, x2, moe_w_gate[0].astype(BF16), moe_w_up[0].astype(BF16),
                 moe_w_down[0].astype(BF16))
    out = _combine(y, wts, x2, vec(ln3_g[0]), vec(ln3_b[0]))
    stages = dict(q=q, kc=kc, vc=vc, o_nsa=o_nsa, o_conv=o_conv, x1=x1, x2=x2)
    return out.reshape(B, S, D), stages


def kernel(x, mem, ln_in_g, ln_in_b, w_in, cmp_pe_k, cmp_w1_k, cmp_w2_k, cmp_pe_v, cmp_w1_v, cmp_w2_v, rel_table, conv_dw_w, conv_dw_b, conv_ln_g, conv_ln_b, w_out, ln1_g, ln1_b, mem_ln_g, mem_ln_b, xa_wq, xa_wkv, xa_wo, ln2_g, ln2_b, router_group_w, router_group_b, router_expert_w, router_expert_b, moe_w_gate, moe_w_up, moe_w_down, ln3_g, ln3_b):
    out, _ = _forward(x, mem, ln_in_g, ln_in_b, w_in, cmp_pe_k, cmp_w1_k, cmp_w2_k, cmp_pe_v, cmp_w1_v, cmp_w2_v, rel_table, conv_dw_w, conv_dw_b, conv_ln_g, conv_ln_b, w_out, ln1_g, ln1_b, mem_ln_g, mem_ln_b, xa_wq, xa_wkv, xa_wo, ln2_g, ln2_b, router_group_w, router_group_b, router_expert_w, router_expert_b, moe_w_gate, moe_w_up, moe_w_down, ln3_g, ln3_b)
    return out
```

```python
import functools
import math

import jax
import jax.numpy as jnp
import numpy as np
from jax import lax
from jax.experimental import pallas as pl
from jax.experimental.pallas import tpu as pltpu

HEAD_DIM = 64
NSA_HEADS = 16
NSA_KV_HEADS = 4
NSA_GROUP = 4
CMP_BLOCK = 32
CMP_STRIDE = 16
CMP_HIDDEN = 256
SEL_BLOCK = 64
SEL_TOPK = 16
WIN = 512
Q_BLOCK = 128
CONV_K = 31
REL_BUCKETS = 32
REL_MAX_DIST = 128
X_HEADS = 4
MOE_GROUPS = 4
MOE_EPG = 8
MOE_EXPERTS = 32
LN_EPS = 1e-5
DEPTH = 1
ALPHA = (2 * DEPTH) ** 0.25

LANES = 128
SUBLANES = 8
VMEM_LIMIT = 56 * 1024 * 1024
ROW_TILE = 256
MOE_ROWS = 256

STEP_BLOCKS = 2
STEP_Q = STEP_BLOCKS * Q_BLOCK
CMP_FRAME = 512
SEL_FRAME = 128
KEY_PAD = 512
FAR_CHUNK = 512
NEG = -float(2 ** 30)
M_FLOOR = -float(2 ** 27)
LOG2E = math.log2(math.e)
COL_ONE, COL_PAD, COL_ONE2 = 64, 65, 66

F32 = jnp.float32
BF16 = jnp.bfloat16


def _dot(a, b):
    return jnp.dot(a, b, preferred_element_type=F32)


def _dot_nt(a, b):
    return lax.dot_general(a, b, (((1,), (1,)), ((), ())), preferred_element_type=F32)


def _ln(x, g, b):
    mu = jnp.mean(x, axis=-1, keepdims=True)
    xc = x - mu
    var = jnp.mean(xc * xc, axis=-1, keepdims=True)
    return xc * lax.rsqrt(var + LN_EPS) * g + b


def _params(sem):
    return pltpu.CompilerParams(dimension_semantics=sem, vmem_limit_bytes=VMEM_LIMIT)


def _const_spec(shape):
    nd = len(shape)
    return pl.BlockSpec(shape, lambda *_: (0,) * nd)


def _in_proj_kernel(x_ref, g_ref, b_ref, wq_ref, wkv_ref, wg_ref, wa_ref, wgt_ref,
                    q_ref, kv_ref, gate_ref, glu_ref):
    h = _ln(x_ref[...], g_ref[...], b_ref[...]).astype(BF16)
    q_ref[...] = (_dot(h, wq_ref[...]) * (HEAD_DIM ** -0.5 * LOG2E)).astype(BF16)
    kv_ref[...] = _dot(h, wkv_ref[...]).astype(BF16)
    gate_ref[...] = jax.nn.sigmoid(_dot(h, wg_ref[...]))
    glu_ref[...] = _dot(h, wa_ref[...]) * jax.nn.sigmoid(_dot(h, wgt_ref[...]))


def _in_proj(x2d, g, b, wq, wkv, wg, wa, wgt):
    T, D = x2d.shape
    tm = ROW_TILE
    row = lambda n: pl.BlockSpec((tm, n), lambda i: (i, 0))
    return pl.pallas_call(
        _in_proj_kernel,
        grid=(T // tm,),
        in_specs=[row(D), _const_spec((1, D)), _const_spec((1, D)), _const_spec(wq.shape),
                  _const_spec(wkv.shape), _const_spec(wg.shape), _const_spec(wa.shape),
                  _const_spec(wgt.shape)],
        out_specs=[row(wq.shape[1]), row(wkv.shape[1]), row(wg.shape[1]), row(wa.shape[1])],
        out_shape=[jax.ShapeDtypeStruct((T, wq.shape[1]), BF16),
                   jax.ShapeDtypeStruct((T, wkv.shape[1]), BF16),
                   jax.ShapeDtypeStruct((T, wg.shape[1]), F32),
                   jax.ShapeDtypeStruct((T, wa.shape[1]), F32)],
        compiler_params=_params(("parallel",)),
        name="in_proj",
    )(x2d, g, b, wq, wkv, wg, wa, wgt)


def _compress_kernel(ch_ref, pe_ref, w1a_ref, w1b_ref, w1_ref, w2_ref, tail_ref, out_ref, *, n_rows):
    ch = ch_ref[0, 0]
    first = _dot(ch, w1a_ref[...])
    second = _dot(ch, w1b_ref[...])
    second = pltpu.roll(second, n_rows - 1, 0)
    pe_term = _dot(pe_ref[...], w1_ref[...])[0:1, :]
    hid = jax.nn.gelu(first + second + pe_term).astype(BF16)
    res = _dot(hid, w2_ref[...]) + tail_ref[0:1, :]
    out_ref[0, 0, 0:CMP_FRAME, :] = jnp.broadcast_to(tail_ref[1:2, :], (CMP_FRAME, LANES))
    out_ref[0, 0, CMP_FRAME:CMP_FRAME + n_rows, :] = res


def _compress(ch, pe8, w1a, w1b, w1, w2p, tail):
    B, Hk, n_rows, width = ch.shape
    kern = functools.partial(_compress_kernel, n_rows=n_rows)
    return pl.pallas_call(
        kern,
        grid=(B, Hk),
        in_specs=[pl.BlockSpec((1, 1, n_rows, width), lambda b, h: (b, h, 0, 0)),
                  _const_spec(pe8.shape), _const_spec(w1a.shape), _const_spec(w1b.shape),
                  _const_spec(w1.shape), _const_spec(w2p.shape), _const_spec(tail.shape)],
        out_specs=pl.BlockSpec((1, 1, CMP_FRAME + n_rows, LANES), lambda b, h: (b, h, 0, 0)),
        out_shape=jax.ShapeDtypeStruct((B, Hk, CMP_FRAME + n_rows, LANES), F32),
        compiler_params=_params(("parallel", "parallel")),
        name="compress",
    )(ch, pe8, w1a, w1b, w1, w2p, tail)


def _softmax_update(carry, s, v):
    m, acc = carry
    m_new = jnp.maximum(m, jnp.max(s, axis=-1, keepdims=True))
    p = jnp.exp2(s - m_new).astype(BF16)
    acc = acc * jnp.exp2(m - m_new) + _dot(p, v)
    return m_new, acc


def _softmax_merge(a, b):
    m = jnp.maximum(a[0], b[0])
    return m, a[1] * jnp.exp2(a[0] - m) + b[1] * jnp.exp2(b[0] - m)


def _add_on_lanes(s, lo, tile):
    hi = lo + tile.shape[1]
    parts = [s[:, :lo]] if lo else []
    parts.append(s[:, lo:hi] + tile)
    if hi < s.shape[1]:
        parts.append(s[:, hi:])
    return jnp.concatenate(parts, axis=1)


def _inv_row_sum(acc):
    return 1.0 / jnp.maximum(acc[:, COL_ONE:COL_ONE + 1], 1e-30)


def _nsa_kernel(q_ref, gate_ref, kc_ref, vc_ref, ks_ref, vs_ref, kw_ref, vw_ref,
                eye_ref, ovt_ref, wt_ref, nt_ref, ct_ref, qx_ref, pick_ref, place_ref, out_ref):
    si = pl.program_id(2)
    NB, G, QB = STEP_BLOCKS, NSA_GROUP, Q_BLOCK
    rows = NB * G * QB
    groups = [(b, g) for b in range(NB) for g in range(G)]
    row_slice = lambda b, g: slice((b * G + g) * QB, (b * G + g + 1) * QB)
    init = (jnp.full((rows, 1), M_FLOOR, F32), jnp.zeros((rows, LANES), F32))

    qblk = q_ref[0]
    q_pad = jnp.concatenate([_dot(qblk[b * QB:(b + 1) * QB], pick_ref[g]) for b, g in groups], axis=0)
    q_pad = (q_pad + qx_ref[0]).astype(BF16)
    q0 = si * STEP_Q

    r_win = q0 - WIN + KEY_PAD
    carry_w = init
    for lo in range(0, WIN + STEP_Q, FAR_CHUNK):
        n = min(FAR_CHUNK, WIN + STEP_Q - lo)
        r = pl.multiple_of(r_win + lo, LANES)
        s_w = _dot_nt(q_pad, kw_ref[0, 0, pl.ds(r, n), :]) + wt_ref[0, :, lo:lo + n]
        carry_w = _softmax_update(carry_w, s_w, vw_ref[0, 0, pl.ds(r, n), :])
    acc_w = carry_w[1]

    c0 = pl.multiple_of(SUBLANES * NB * (si + 1), SUBLANES)
    kc = kc_ref[0, 0, pl.ds(c0, CMP_FRAME), :].astype(BF16)
    vc = vc_ref[0, 0, pl.ds(c0, CMP_FRAME), :].astype(BF16)
    s_c = _add_on_lanes(_dot_nt(q_pad, kc), CMP_FRAME - LANES, ct_ref[0])
    m_c = jnp.maximum(jnp.max(s_c, axis=-1, keepdims=True), M_FLOOR)
    e_c = jnp.exp2(s_c - m_c)
    p_c = e_c * (1.0 / jnp.maximum(jnp.sum(e_c, axis=-1, keepdims=True), 1e-30))
    o_c = _dot(p_c.astype(BF16), vc)

    p_sum = jnp.concatenate([sum(p_c[row_slice(b, g)] for g in range(G)) for b in range(NB)], axis=0)
    p_hi = p_sum.astype(BF16)
    p_lo = (p_sum - p_hi.astype(F32)).astype(BF16)
    imp_t = _dot_nt(ovt_ref[...], p_hi) + _dot_nt(ovt_ref[...], p_lo)
    jr = lax.broadcasted_iota(jnp.int32, (SEL_FRAME, STEP_Q), 0)
    ql = lax.broadcasted_iota(jnp.int32, (SEL_FRAME, STEP_Q), 1)
    j_abs = jr + (2 * NB * (si + 1) - SEL_FRAME)
    cur = (SEL_FRAME - 2 * NB) + ql // SEL_BLOCK
    exists = j_abs >= 0
    forced = exists & ((j_abs == 0) | (jr == cur) | (jr == cur - 1))
    valid = exists & (jr <= cur)
    free = valid & jnp.logical_not(forced)
    big = 3.0e38
    score = jnp.where(free, imp_t, -big)
    chosen = jnp.zeros((SEL_FRAME, STEP_Q), jnp.bool_)
    for _ in range(SEL_TOPK - 3):
        top = jnp.max(score, axis=0, keepdims=True)
        first = jnp.min(jnp.where(score == top, jr, SEL_FRAME), axis=0, keepdims=True)
        hit = jr == first
        chosen = chosen | hit
        score = jnp.where(hit, -big, score)
    neg_t = jnp.where(forced | (chosen & free), 0.0, NEG)
    neg = neg_t.T.astype(BF16)
    neg_rows = jnp.concatenate([neg[b * QB:(b + 1) * QB] for b, _ in groups], axis=0)
    q_sel = jnp.concatenate([neg_rows, q_pad], axis=1)

    near_blocks = 2 * STEP_Q // SEL_BLOCK
    far_end = q0 - STEP_Q + KEY_PAD
    n_far = (jnp.maximum(q0 - STEP_Q, 0) + FAR_CHUNK - 1) // FAR_CHUNK

    def far_chunk(chain, i, live):
        r0 = pl.multiple_of(jnp.where(live, far_end - FAR_CHUNK * (i + 1), 0), LANES)
        e0 = jnp.where(live, SEL_BLOCK * (SEL_FRAME - near_blocks) - FAR_CHUNK * (i + 1) + KEY_PAD, 0)
        e0 = pl.multiple_of(e0, LANES)
        keys = jnp.concatenate([eye_ref[pl.ds(e0, FAR_CHUNK), :], ks_ref[0, 0, pl.ds(r0, FAR_CHUNK), :]], axis=1)
        s = _dot_nt(q_sel, keys)
        return _softmax_update(chain, s, vs_ref[0, 0, pl.ds(r0, FAR_CHUNK), :])

    def far_pair(i, carry):
        even, odd = carry
        return far_chunk(even, 2 * i, True), far_chunk(odd, 2 * i + 1, 2 * i + 1 < n_far)

    even, odd = lax.fori_loop(0, (n_far + 1) // 2, far_pair, (init, init))
    carry = _softmax_merge(even, odd)
    r_near = pl.multiple_of(far_end, LANES)
    e_near = SEL_BLOCK * (SEL_FRAME - near_blocks) + KEY_PAD
    keys = jnp.concatenate([eye_ref[e_near:e_near + 2 * STEP_Q, :], ks_ref[0, 0, pl.ds(r_near, 2 * STEP_Q), :]],
                           axis=1)
    s_n = _dot_nt(q_sel, keys) + nt_ref[0]
    _, acc_s = _softmax_update(carry, s_n, vs_ref[0, 0, pl.ds(r_near, 2 * STEP_Q), :])

    gates = gate_ref[0]
    def gate_col(br):
        return jnp.concatenate([gates[b * QB:(b + 1) * QB, 3 * g + br:3 * g + br + 1] for b, g in groups], axis=0)
    o = (gate_col(0) * o_c + (gate_col(1) * _inv_row_sum(acc_s)) * acc_s
         + (gate_col(2) * _inv_row_sum(acc_w)) * acc_w).astype(BF16)
    out = [sum(_dot(o[row_slice(b, g)], place_ref[g]) for g in range(G)) for b in range(NB)]
    out_ref[0] = jnp.concatenate(out, axis=0).astype(BF16)


def _nsa(q, gates, kc, vc, ks, vs, kw, vw, eye, ovt, wt, nt, ct, qx, pick, place):
    B, S, _ = q.shape
    Hk = NSA_KV_HEADS
    per_head = lambda a: pl.BlockSpec((1, 1) + a.shape[2:], lambda b, h, i: (b, h, 0, 0))
    tab = lambda a: pl.BlockSpec((1,) + a.shape[1:], lambda b, h, i: (h, 0, 0))
    qspec = pl.BlockSpec((1, STEP_Q, NSA_GROUP * HEAD_DIM), lambda b, h, i: (b, i, h))
    return pl.pallas_call(
        _nsa_kernel,
        grid=(B, Hk, S // STEP_Q),
        in_specs=[qspec, pl.BlockSpec((1, STEP_Q, LANES), lambda b, h, i: (b, i, h)),
                  per_head(kc), per_head(vc), per_head(ks), per_head(vs), per_head(kw), per_head(vw),
                  _const_spec(eye.shape), _const_spec(ovt.shape), tab(wt), tab(nt), tab(ct), tab(qx),
                  _const_spec(pick.shape), _const_spec(place.shape)],
        out_specs=qspec,
        out_shape=jax.ShapeDtypeStruct(q.shape, BF16),
        compiler_params=_params(("parallel", "parallel", "arbitrary")),
        name="nsa",
    )(q, gates, kc, vc, ks, vs, kw, vw, eye, ovt, wt, nt, ct, qx, pick, place)


CONV_HALO = 32


def _conv_kernel(cur_ref, prev_ref, w_ref, b_ref, g_ref, beta_ref, out_ref):
    i = pl.program_id(1)
    cur = cur_ref[0]
    rows = cur.shape[0]
    halo = prev_ref[0, rows - CONV_HALO:rows, :]
    halo = jnp.where(i > 0, halo, 0.0)
    ext = jnp.concatenate([halo, cur], axis=0)
    acc = b_ref[...]
    for s in range(SUBLANES):
        part = 0.0
        for a in range((CONV_K - 1 - s) // SUBLANES + 1):
            lo = CONV_HALO - SUBLANES * (a + 1)
            k = CONV_K - 1 - SUBLANES * a - s
            part = part + ext[lo:lo + rows + SUBLANES, :] * w_ref[k:k + 1, :]
        acc = acc + part[SUBLANES - s:SUBLANES - s + rows, :]
    y = _ln(acc, g_ref[...], beta_ref[...])
    out_ref[0] = (y * jax.nn.sigmoid(y)).astype(BF16)


def _conv(hglu, w, b, g, beta):
    B, S, C = hglu.shape
    rows = ROW_TILE
    return pl.pallas_call(
        _conv_kernel,
        grid=(B, S // rows),
        in_specs=[pl.BlockSpec((1, rows, C), lambda b, i: (b, i, 0)),
                  pl.BlockSpec((1, rows, C), lambda b, i: (b, jnp.maximum(i - 1, 0), 0)),
                  _const_spec(w.shape), _const_spec(b.shape), _const_spec(g.shape), _const_spec(beta.shape)],
        out_specs=pl.BlockSpec((1, rows, C), lambda b, i: (b, i, 0)),
        out_shape=jax.ShapeDtypeStruct((B, S, C), BF16),
        compiler_params=_params(("parallel", "arbitrary")),
        name="conv",
    )(hglu, hglu, w, b, g, beta)


def _out_proj_kernel(nsa_ref, conv_ref, x_ref, gin_ref, bin_ref, wa_ref, wb_ref, g_ref, b_ref, out_ref):
    h = _ln(x_ref[...], gin_ref[...], bin_ref[...])
    mix = _dot(nsa_ref[...], wa_ref[...]) + _dot(conv_ref[...], wb_ref[...])
    out_ref[...] = _ln(ALPHA * h + mix, g_ref[...], b_ref[...])


def _out_proj(o_nsa, o_conv, x2d, gin, bin_, wa, wb, g, b):
    T, D = x2d.shape
    tm = ROW_TILE
    row = lambda n: pl.BlockSpec((tm, n), lambda i: (i, 0))
    vec = _const_spec((1, D))
    return pl.pallas_call(
        _out_proj_kernel,
        grid=(T // tm,),
        in_specs=[row(o_nsa.shape[1]), row(o_conv.shape[1]), row(D), vec, vec,
                  _const_spec(wa.shape), _const_spec(wb.shape), vec, vec],
        out_specs=row(D),
        out_shape=jax.ShapeDtypeStruct((T, D), F32),
        compiler_params=_params(("parallel",)),
        name="out_proj",
    )(o_nsa, o_conv, x2d, gin, bin_, wa, wb, g, b)


def _mem_kv_kernel(m_ref, g_ref, b_ref, w_ref, out_ref):
    out_ref[...] = _dot(_ln(m_ref[...], g_ref[...], b_ref[...]).astype(BF16), w_ref[...]).astype(BF16)


def _mem_kv(mem2d, g, b, wkv):
    R, D = mem2d.shape
    N = wkv.shape[1]
    tn = 1024
    return pl.pallas_call(
        _mem_kv_kernel,
        grid=(R // ROW_TILE, N // tn),
        in_specs=[pl.BlockSpec((ROW_TILE, D), lambda i, j: (i, 0)), _const_spec((1, D)), _const_spec((1, D)),
                  pl.BlockSpec((D, tn), lambda i, j: (0, j))],
        out_specs=pl.BlockSpec((ROW_TILE, tn), lambda i, j: (i, j)),
        out_shape=jax.ShapeDtypeStruct((R, N), BF16),
        compiler_params=_params(("parallel", "parallel")),
        name="mem_kv",
    )(mem2d, g, b, wkv)


def _xattn_kernel(x_ref, kv_ref, wq_ref, wo_ref, g_ref, b_ref, rw_ref, rw_hi_ref, rb_ref,
                  x2_ref, idx_ref, wt_ref):
    D = x_ref.shape[1]
    dh = D // X_HEADS
    x1 = x_ref[...]
    q = _dot(x1.astype(BF16), wq_ref[...]).astype(BF16)
    heads = []
    for h in range(X_HEADS):
        k = kv_ref[0, :, h * dh:(h + 1) * dh]
        v = kv_ref[0, :, D + h * dh:D + (h + 1) * dh]
        s = _dot_nt(q[:, h * dh:(h + 1) * dh], k) * (dh ** -0.5)
        e = jnp.exp(s - jnp.max(s, axis=-1, keepdims=True))
        p = e / jnp.sum(e, axis=-1, keepdims=True)
        heads.append(_dot(p.astype(BF16), v).astype(BF16))
    xa = _dot(jnp.concatenate(heads, axis=1), wo_ref[...])
    x2 = _ln(ALPHA * x1 + xa, g_ref[...], b_ref[...])
    x2_ref[...] = x2

    x_hi = x2.astype(BF16)
    x_lo = (x2 - x_hi.astype(F32)).astype(BF16)
    both = _dot(x_hi, rw_ref[...])
    logit = both[:, 0:LANES] + both[:, LANES:2 * LANES] + _dot(x_lo, rw_hi_ref[...]) + rb_ref[...]
    lane = lax.broadcasted_iota(jnp.int32, logit.shape, 1)
    big = 3.0e38
    lg = jnp.where(lane < MOE_GROUPS, logit, -big)
    lg_max = jnp.max(lg, axis=-1, keepdims=True)
    grp = jnp.min(jnp.where(lg == lg_max, lane, LANES), axis=-1, keepdims=True)
    w_grp = 1.0 / jnp.sum(jnp.exp(lg - lg_max), axis=-1, keepdims=True)
    e_lane = lane - MOE_GROUPS
    in_grp = (e_lane >= grp * MOE_EPG) & (e_lane < (grp + 1) * MOE_EPG)
    le = jnp.where(in_grp, logit, -big)
    v1 = jnp.max(le, axis=-1, keepdims=True)
    i1 = jnp.min(jnp.where(le == v1, lane, LANES), axis=-1, keepdims=True)
    le2 = jnp.where(lane == i1, -big, le)
    v2 = jnp.max(le2, axis=-1, keepdims=True)
    i2 = jnp.min(jnp.where(le2 == v2, lane, LANES), axis=-1, keepdims=True)
    r = jnp.exp(v2 - v1)
    w1 = w_grp / (1.0 + r)
    w2 = w_grp * r / (1.0 + r)
    idx_ref[...] = jnp.where(lane == 0, i1 - MOE_GROUPS, jnp.where(lane == 1, i2 - MOE_GROUPS, 0))
    wt_ref[...] = jnp.where(lane == 0, w1, jnp.where(lane == 1, w2, 0.0))


def _xattn(x1, mem_kv, S, wq, wo, g, b, rw, rw_hi, rb):
    T, D = x1.shape
    tm = ROW_TILE
    row = lambda n: pl.BlockSpec((tm, n), lambda i: (i, 0))
    vec = _const_spec((1, D))
    per_step = S // tm
    return pl.pallas_call(
        _xattn_kernel,
        grid=(T // tm,),
        in_specs=[row(D), pl.BlockSpec((1,) + mem_kv.shape[1:], lambda i: (i // per_step, 0, 0)),
                  _const_spec(wq.shape), _const_spec(wo.shape), vec, vec,
                  _const_spec(rw.shape), _const_spec(rw_hi.shape), _const_spec(rb.shape)],
        out_specs=[row(D), row(LANES), row(LANES)],
        out_shape=[jax.ShapeDtypeStruct((T, D), F32), jax.ShapeDtypeStruct((T, LANES), jnp.int32),
                   jax.ShapeDtypeStruct((T, LANES), F32)],
        compiler_params=_params(("parallel",)),
        name="xattn",
    )(x1, mem_kv, wq, wo, g, b, rw, rw_hi, rb)


def _gather_copy(x_hbm, a, xbuf, sem, r):
    tok = jnp.maximum(a, 0) >> 1
    return pltpu.make_async_copy(x_hbm.at[pl.ds(tok, 1), :], xbuf.at[pl.ds(r, 1), :], sem)


def _scatter_copy(ybuf, a, y_hbm, sem, r, n_tok, width, slot):
    real = a >= 0
    row = jnp.where(real, a >> 1, n_tok + r)
    col = pl.multiple_of(jnp.where(real, a & 1, slot) * width, width)
    return pltpu.make_async_copy(ybuf.at[pl.ds(r, 1), :], y_hbm.at[pl.ds(row, 1), pl.ds(col, width)], sem)


def _moe_ffn_kernel(blk_e_ref, prev_ref, cur_ref, nxt_ref, x_hbm, wg_ref, wu_ref, wd_ref, y_hbm,
                    xbuf, ybuf, gsem, ssem):
    i = pl.program_id(0)
    last = pl.num_programs(0) - 1
    slot = i % 2
    other = 1 - slot
    R = MOE_ROWS
    n_tok, width = x_hbm.shape
    rows = range(R)
    gather = lambda ref, s, r: _gather_copy(x_hbm, ref[0, 0, r], xbuf.at[s], gsem.at[s], r)
    scatter = lambda a, s, r: _scatter_copy(ybuf.at[s], a, y_hbm, ssem.at[s], r, n_tok, width, s)
    gather_wait = lambda s: pltpu.make_async_copy(xbuf.at[s], xbuf.at[s], gsem.at[s]).wait()
    scatter_wait = lambda s: pltpu.make_async_copy(ybuf.at[s], ybuf.at[s], ssem.at[s]).wait()

    @pl.when(i == 0)
    def _():
        ybuf[...] = jnp.zeros_like(ybuf)
        for r in rows:
            gather(cur_ref, 0, r).start()
            scatter(-1, 0, r).start()

    gather_wait(slot)
    scatter_wait(slot)
    for r in rows:
        gather(nxt_ref, other, r).start()
        scatter(jnp.where(i > 0, prev_ref[0, 0, r], -1), other, r).start()

    xb = xbuf[slot].astype(BF16)
    hid = jax.nn.silu(_dot(xb, wg_ref[0])) * _dot(xb, wu_ref[0])
    ybuf[slot] = _dot(hid.astype(BF16), wd_ref[0])

    @pl.when(i == last)
    def _():
        for r in rows:
            scatter(cur_ref[0, 0, r], slot, r).start()
        gather_wait(other)
        scatter_wait(other)
        scatter_wait(slot)


def _moe_ffn(a_buf, blk_e, x2, w_gate, w_up, w_down):
    T, D = x2.shape
    R = MOE_ROWS
    n_blk = a_buf.shape[0] // R
    Hd = w_gate.shape[2]
    a3 = a_buf.reshape(n_blk, 1, R)
    idx = lambda f: pl.BlockSpec((1, 1, R), lambda i, be: (f(i), 0, 0), memory_space=pltpu.SMEM)
    grid_spec = pltpu.PrefetchScalarGridSpec(
        num_scalar_prefetch=1,
        grid=(n_blk,),
        in_specs=[idx(lambda i: jnp.maximum(i - 1, 0)), idx(lambda i: i), idx(lambda i: jnp.minimum(i + 1, n_blk - 1)),
                  pl.BlockSpec(memory_space=pl.ANY),
                  pl.BlockSpec((1, D, Hd), lambda i, be: (be[i], 0, 0)),
                  pl.BlockSpec((1, D, Hd), lambda i, be: (be[i], 0, 0)),
                  pl.BlockSpec((1, Hd, D), lambda i, be: (be[i], 0, 0))],
        out_specs=pl.BlockSpec(memory_space=pl.ANY),
        scratch_shapes=[pltpu.VMEM((2, R, D), F32), pltpu.VMEM((2, R, D), F32),
                        pltpu.SemaphoreType.DMA((2,)), pltpu.SemaphoreType.DMA((2,))],
    )
    return pl.pallas_call(
        _moe_ffn_kernel,
        grid_spec=grid_spec,
        out_shape=jax.ShapeDtypeStruct((T + R, 2 * D), F32),
        compiler_params=_params(("arbitrary",)),
        name="moe_ffn",
    )(blk_e, a3, a3, a3, x2, w_gate, w_up, w_down)


def _combine_kernel(y_ref, wt_ref, x_ref, g_ref, b_ref, out_ref):
    D = x_ref.shape[1]
    wt = wt_ref[...]
    moe = wt[:, 0:1] * y_ref[:, 0:D] + wt[:, 1:2] * y_ref[:, D:2 * D]
    out_ref[...] = _ln(ALPHA * x_ref[...] + moe, g_ref[...], b_ref[...])


def _combine(y, wts, x2, g, b):
    T, D = x2.shape
    tm = ROW_TILE
    row = lambda n: pl.BlockSpec((tm, n), lambda i: (i, 0))
    return pl.pallas_call(
        _combine_kernel,
        grid=(T // tm,),
        in_specs=[row(2 * D), row(LANES), row(D), _const_spec((1, D)), _const_spec((1, D))],
        out_specs=row(D),
        out_shape=jax.ShapeDtypeStruct((T, D), F32),
        compiler_params=_params(("parallel",)),
        name="moe_combine",
    )(y, wts, x2, g, b)


def _t5_bucket(dist):
    n = np.maximum(dist, 0)
    max_exact = REL_BUCKETS // 2
    nf = np.maximum(n, 1).astype(np.float64)
    large = max_exact + (np.log(nf / max_exact) / math.log(REL_MAX_DIST / max_exact)
                         * (REL_BUCKETS - max_exact)).astype(np.int64)
    large = np.minimum(large, REL_BUCKETS - 1)
    return np.where(n < max_exact, n, large)


def _bias_tiles(rel_table):
    Hk, G, QB, NB = NSA_KV_HEADS, NSA_GROUP, Q_BLOCK, STEP_BLOCKS
    tab = rel_table.reshape(REL_BUCKETS, Hk, G).transpose(1, 2, 0) * LOG2E
    far = tab[:, :, REL_BUCKETS - 1]
    far_hi = far.astype(BF16).astype(F32)
    far_lo = far - far_hi
    rel = tab - far[:, :, None]

    def sub_tile(dist, valid):
        bucket = _t5_bucket(dist)
        if not (valid & (bucket != REL_BUCKETS - 1)).any():
            return jnp.broadcast_to(jnp.asarray(np.where(valid, 0.0, NEG), F32), (Hk, G) + dist.shape)
        onehot = jnp.asarray(bucket[None] == np.arange(REL_BUCKETS)[:, None, None], F32)
        bias = jnp.einsum('hgk,kqn->hgqn', rel, onehot, precision=lax.Precision.HIGHEST)
        return jnp.where(jnp.asarray(valid), bias, NEG)

    def tile(dist_fn, valid_fn, n_keys):
        qr = np.arange(QB)[:, None]
        kk = np.arange(LANES)[None, :]
        blocks = []
        for b in range(NB):
            subs = []
            for c in range(n_keys // LANES):
                dist = dist_fn(b, qr, kk + c * LANES)
                subs.append(sub_tile(dist, valid_fn(dist)))
            blocks.append(jnp.concatenate(subs, axis=-1))
        return jnp.stack(blocks, axis=1).reshape(Hk, NB * G * QB, n_keys).astype(F32)

    wt = tile(lambda b, qr, kr: WIN + QB * b + qr - kr, lambda d: (d >= 0) & (d < WIN), WIN + STEP_Q)
    nt = tile(lambda b, qr, kr: STEP_Q + QB * b + qr - kr, lambda d: d >= 0, 2 * STEP_Q)
    frame_lo = CMP_FRAME - LANES
    ct = tile(lambda b, qr, r: (QB * (b - NB) + qr + CMP_STRIDE * (CMP_FRAME - frame_lo - r) - (CMP_BLOCK - 1)),
              lambda d: d >= 0, LANES)
    lane = np.arange(LANES)[None, None, None, :]
    qx = (jnp.where(lane == COL_ONE, far_hi[:, :, None, None], 0.0)
          + jnp.where(lane == COL_ONE2, far_lo[:, :, None, None], 0.0)
          + jnp.where(lane == COL_PAD, NEG, 0.0))
    qx = jnp.broadcast_to(qx[:, None], (Hk, NB, G, QB, LANES)).reshape(Hk, NB * G * QB, LANES)
    return wt, nt, ct, qx.astype(F32)


def _static_tables():
    G = NSA_GROUP
    rows = np.arange(SEL_FRAME * SEL_BLOCK)
    eye = (rows[:, None] // SEL_BLOCK == np.arange(SEL_FRAME)[None, :]).astype(np.float32)
    eye = np.concatenate([np.zeros((KEY_PAD, SEL_FRAME), np.float32), eye], axis=0)
    cs = np.arange(CMP_FRAME) * CMP_STRIDE
    ss = np.arange(SEL_FRAME) * SEL_BLOCK
    ov = np.clip(np.minimum(cs[:, None] + CMP_BLOCK, ss[None, :] + SEL_BLOCK)
                 - np.maximum(cs[:, None], ss[None, :]), 0, None) / CMP_BLOCK
    pick = np.zeros((G, G * HEAD_DIM, LANES), np.float32)
    place = np.zeros((G, LANES, G * HEAD_DIM), np.float32)
    for g in range(G):
        for d in range(HEAD_DIM):
            pick[g, g * HEAD_DIM + d, d] = 1.0
            place[g, d, g * HEAD_DIM + d] = 1.0
    return (jnp.asarray(eye, BF16), jnp.asarray(ov.T, BF16), jnp.asarray(pick, BF16), jnp.asarray(place, BF16))


def _key_layout(t, B, S, ones_cols, pad_col):
    Hk = NSA_KV_HEADS
    t = t.reshape(B, S, Hk, HEAD_DIM).transpose(0, 2, 1, 3)
    lane = np.arange(LANES - HEAD_DIM) + HEAD_DIM
    extra = np.isin(lane, ones_cols).astype(np.float32)
    body = jnp.concatenate([t, jnp.broadcast_to(jnp.asarray(extra, t.dtype), (B, Hk, S, LANES - HEAD_DIM))], axis=-1)
    pad_row = np.zeros((LANES,), np.float32)
    if pad_col is not None:
        pad_row[pad_col] = 1.0
    pad = jnp.broadcast_to(jnp.asarray(pad_row, t.dtype), (B, Hk, KEY_PAD, LANES))
    return jnp.concatenate([pad, body], axis=2)


def _dispatch(idx, T):
    R, E = MOE_ROWS, MOE_EXPERTS
    A = 2 * T
    id_bits = max(A - 1, 1).bit_length()
    e_flat = idx[:, 0:2].reshape(A)
    counts = jnp.sum((e_flat[:, None] == jnp.arange(E, dtype=jnp.int32)[None, :]).astype(jnp.int32), axis=0)
    n_pad = (-counts) % R
    real = (e_flat << (id_bits + 1)) | jnp.arange(A, dtype=jnp.int32)
    d_e = jnp.arange(E, dtype=jnp.int32)[:, None]
    d_r = jnp.arange(R, dtype=jnp.int32)[None, :]
    pad = jnp.where(d_r < n_pad[:, None], (d_e << (id_bits + 1)) | (1 << id_bits) | d_r,
                    (E << (id_bits + 1)) | (d_e * R + d_r))
    keys = jnp.sort(jnp.concatenate([real, pad.reshape(E * R)]))
    expert = keys >> (id_bits + 1)
    is_pad = ((keys >> id_bits) & 1) == 1
    a_buf = jnp.where(is_pad | (expert >= E), -1, keys & ((1 << id_bits) - 1))
    blk_e = jnp.minimum(expert.reshape(-1, R)[:, 0], E - 1)
    return a_buf, blk_e


def _forward(x, mem, ln_in_g, ln_in_b, w_in, cmp_pe_k, cmp_w1_k, cmp_w2_k, cmp_pe_v, cmp_w1_v, cmp_w2_v, rel_table, conv_dw_w, conv_dw_b, conv_ln_g, conv_ln_b, w_out, ln1_g, ln1_b, mem_ln_g, mem_ln_b, xa_wq, xa_wkv, xa_wo, ln2_g, ln2_b, router_group_w, router_group_b, router_expert_w, router_expert_b, moe_w_gate, moe_w_up, moe_w_down, ln3_g, ln3_b):
    B, S, D = x.shape
    T = B * S
    Hk, G, dh = NSA_KV_HEADS, NSA_GROUP, HEAD_DIM
    assert S % ROW_TILE == 0 and S % STEP_Q == 0 and SEL_TOPK * SEL_BLOCK <= S <= SEL_FRAME * SEL_BLOCK
    vec = lambda v: v.reshape(1, -1).astype(F32)
    x2d = x.reshape(T, D)

    w = w_in[0]
    nq, nkv = NSA_HEADS * dh, Hk * dh
    c_g = nq + 6 * nkv
    c_u = c_g + NSA_HEADS * 3
    cw = (w.shape[1] - c_u) // 2
    wq = w[:, :nq].astype(BF16)
    wkv = w[:, nq:c_g].astype(BF16)
    wg = jnp.pad(w[:, c_g:c_u].reshape(D, Hk, G * 3), ((0, 0), (0, 0), (0, LANES - G * 3)))
    wg = wg.reshape(D, Hk * LANES).astype(BF16)
    wa = w[:, c_u:c_u + cw].astype(BF16)
    wgt = w[:, c_u + cw:].astype(BF16)
    q, kv, gates, hglu = _in_proj(x2d, vec(ln_in_g), vec(ln_in_b), wq, wkv, wg, wa, wgt)

    def compressed(col, pe, w1, w2, ones_cols, pad_col):
        t = kv[:, col * nkv:(col + 1) * nkv].reshape(B, S, Hk, dh).transpose(0, 2, 1, 3)
        ch = t.reshape(B, Hk, S // CMP_STRIDE, CMP_STRIDE * dh)
        half = CMP_STRIDE * dh
        pe8 = jnp.pad(pe.reshape(1, CMP_BLOCK * dh), ((0, 7), (0, 0))).astype(BF16)
        w2p = jnp.pad(w2, ((0, 0), (0, LANES - dh))).astype(BF16)
        lane = np.arange(LANES)
        tail = np.zeros((8, LANES), np.float32)
        tail[0] = np.isin(lane, ones_cols)
        if pad_col is not None:
            tail[1, pad_col] = 1.0
        w1b16 = w1.astype(BF16)
        return _compress(ch, pe8, w1b16[:half], w1b16[half:], w1b16, w2p, jnp.asarray(tail))

    kc = compressed(0, cmp_pe_k[0], cmp_w1_k[0], cmp_w2_k[0], (COL_ONE, COL_ONE2), COL_PAD)
    vc = compressed(1, cmp_pe_v[0], cmp_w1_v[0], cmp_w2_v[0], (), None)

    part = lambda c: kv[:, c * nkv:(c + 1) * nkv]
    ks = _key_layout(part(2), B, S, (COL_ONE, COL_ONE2), COL_PAD)
    vs = _key_layout(part(3), B, S, (COL_ONE,), None)
    kw = _key_layout(part(4), B, S, (COL_ONE, COL_ONE2), COL_PAD)
    vw = _key_layout(part(5), B, S, (COL_ONE,), None)
    eye, ovt, pick, place = _static_tables()
    wt, nt, ct, qx = _bias_tiles(rel_table.astype(F32))
    o_nsa = _nsa(q.reshape(B, S, nq), gates.reshape(B, S, Hk * LANES), kc, vc, ks, vs, kw, vw,
                 eye, ovt, wt, nt, ct, qx, pick, place).reshape(T, nq)

    o_conv = _conv(hglu.reshape(B, S, cw), conv_dw_w[0].reshape(CONV_K, cw), vec(conv_dw_b[0]),
                   vec(conv_ln_g[0]), vec(conv_ln_b[0])).reshape(T, cw)

    wo = w_out[0].astype(BF16)
    x1 = _out_proj(o_nsa, o_conv, x2d, vec(ln_in_g), vec(ln_in_b), wo[:nq], wo[nq:], vec(ln1_g[0]), vec(ln1_b[0]))

    mkv = _mem_kv(mem.reshape(-1, D), vec(mem_ln_g[0]), vec(mem_ln_b[0]), xa_wkv[0].astype(BF16))
    mkv = mkv.reshape(B, mem.shape[1], 2 * D)
    rw = jnp.concatenate([router_group_w[0], router_expert_w[0]], axis=1).astype(F32)
    rw = jnp.pad(rw, ((0, 0), (0, LANES - rw.shape[1])))
    rb = jnp.pad(jnp.concatenate([router_group_b[0], router_expert_b[0]]).astype(F32), (0, LANES - MOE_GROUPS - MOE_EXPERTS))
    rw_hi = rw.astype(BF16)
    rw_lo = (rw - rw_hi.astype(F32)).astype(BF16)
    x2, idx, wts = _xattn(x1, mkv, S, xa_wq[0].astype(BF16), xa_wo[0].astype(BF16), vec(ln2_g[0]), vec(ln2_b[0]),
                          jnp.concatenate([rw_hi, rw_lo], axis=1), rw_hi, rb.reshape(1, LANES))

    a_buf, blk_e = _dispatch(idx, T)
    y = _moe_ffn(a_buf, blk_e, x2, moe_w_gate[0].astype(BF16), moe_w_up[0].astype(BF16),
                 moe_w_down[0].astype(BF16))
    out = _combine(y, wts, x2, vec(ln3_g[0]), vec(ln3_b[0]))
    stages = dict(q=q, kc=kc, vc=vc, o_nsa=o_nsa, o_conv=o_conv, x1=x1, x2=x2)
    return out.reshape(B, S, D), stages


def kernel(x, mem, ln_in_g, ln_in_b, w_in, cmp_pe_k, cmp_w1_k, cmp_w2_k, cmp_pe_v, cmp_w1_v, cmp_w2_v, rel_table, conv_dw_w, conv_dw_b, conv_ln_g, conv_ln_b, w_out, ln1_g, ln1_b, mem_ln_g, mem_ln_b, xa_wq, xa_wkv, xa_wo, ln2_g, ln2_b, router_group_w, router_group_b, router_expert_w, router_expert_b, moe_w_gate, moe_w_up, moe_w_down, ln3_g, ln3_b):
    out, _ = _forward(x, mem, ln_in_g, ln_in_b, w_in, cmp_pe_k, cmp_w1_k, cmp_w2_k, cmp_pe_v, cmp_w1_v, cmp_w2_v, rel_table, conv_dw_w, conv_dw_b, conv_ln_g, conv_ln_b, w_out, ln1_g, ln1_b, mem_ln_g, mem_ln_b, xa_wq, xa_wkv, xa_wo, ln2_g, ln2_b, router_group_w, router_group_b, router_expert_w, router_expert_b, moe_w_gate, moe_w_up, moe_w_down, ln3_g, ln3_b)
    return out
```

```python
import functools
import math

import jax
import jax.numpy as jnp
import numpy as np
from jax import lax
from jax.experimental import pallas as pl
from jax.experimental.pallas import tpu as pltpu

HEAD_DIM = 64
NSA_HEADS = 16
NSA_KV_HEADS = 4
NSA_GROUP = 4
CMP_BLOCK = 32
CMP_STRIDE = 16
CMP_HIDDEN = 256
SEL_BLOCK = 64
SEL_TOPK = 16
WIN = 512
Q_BLOCK = 128
CONV_K = 31
REL_BUCKETS = 32
REL_MAX_DIST = 128
X_HEADS = 4
MOE_GROUPS = 4
MOE_EPG = 8
MOE_EXPERTS = 32
LN_EPS = 1e-5
DEPTH = 1
ALPHA = (2 * DEPTH) ** 0.25

LANES = 128
SUBLANES = 8
VMEM_LIMIT = 56 * 1024 * 1024
ROW_TILE = 256
MOE_ROWS = 256

STEP_BLOCKS = 2
STEP_Q = STEP_BLOCKS * Q_BLOCK
CMP_FRAME = 512
SEL_FRAME = 128
KEY_PAD = 512
FAR_CHUNK = 512
NEG = -float(2 ** 30)
M_FLOOR = -float(2 ** 27)
LOG2E = math.log2(math.e)
COL_ONE, COL_PAD, COL_ONE2 = 64, 65, 66

F32 = jnp.float32
BF16 = jnp.bfloat16


def _dot(a, b):
    return jnp.dot(a, b, preferred_element_type=F32)


def _dot_nt(a, b):
    return lax.dot_general(a, b, (((1,), (1,)), ((), ())), preferred_element_type=F32)


def _ln(x, g, b):
    mu = jnp.mean(x, axis=-1, keepdims=True)
    xc = x - mu
    var = jnp.mean(xc * xc, axis=-1, keepdims=True)
    return xc * lax.rsqrt(var + LN_EPS) * g + b


def _pack_halves(x):
    n = x.shape[1] // 2
    bits = lambda v: lax.bitcast_convert_type(v.astype(BF16).astype(F32), jnp.uint32)
    return lax.bitcast_convert_type((bits(x[:, :n]) >> 16) | bits(x[:, n:]), jnp.int32)


def _unpack_halves(w):
    u = lax.bitcast_convert_type(w, jnp.uint32)
    lo = lax.bitcast_convert_type(u << 16, F32).astype(BF16)
    hi = lax.bitcast_convert_type(u & jnp.uint32(0xFFFF0000), F32).astype(BF16)
    return jnp.concatenate([lo, hi], axis=1)


def _rows_to_tiles(ref, packed):
    rows = packed.shape[0]
    for s in range(SUBLANES):
        ref[pl.ds(s, rows, stride=SUBLANES), :] = packed[:, s * LANES:(s + 1) * LANES]


def _tiles_to_rows(ref, rows, first=0, step=SUBLANES):
    return jnp.concatenate([ref[pl.ds(first + s, rows, stride=step), :] for s in range(SUBLANES)], axis=1)


def _params(sem):
    return pltpu.CompilerParams(dimension_semantics=sem, vmem_limit_bytes=VMEM_LIMIT)


def _const_spec(shape):
    nd = len(shape)
    return pl.BlockSpec(shape, lambda *_: (0,) * nd)


def _in_proj_kernel(x_ref, g_ref, b_ref, wq_ref, wkv_ref, wg_ref, wa_ref, wgt_ref,
                    q_ref, kv_ref, gate_ref, glu_ref):
    h = _ln(x_ref[...], g_ref[...], b_ref[...]).astype(BF16)
    q_ref[...] = (_dot(h, wq_ref[...]) * (HEAD_DIM ** -0.5 * LOG2E)).astype(BF16)
    kv_ref[...] = _dot(h, wkv_ref[...]).astype(BF16)
    gate_ref[...] = jax.nn.sigmoid(_dot(h, wg_ref[...]))
    glu_ref[...] = _dot(h, wa_ref[...]) * jax.nn.sigmoid(_dot(h, wgt_ref[...]))


def _in_proj(x2d, g, b, wq, wkv, wg, wa, wgt):
    T, D = x2d.shape
    tm = ROW_TILE
    row = lambda n: pl.BlockSpec((tm, n), lambda i: (i, 0))
    return pl.pallas_call(
        _in_proj_kernel,
        grid=(T // tm,),
        in_specs=[row(D), _const_spec((1, D)), _const_spec((1, D)), _const_spec(wq.shape),
                  _const_spec(wkv.shape), _const_spec(wg.shape), _const_spec(wa.shape),
                  _const_spec(wgt.shape)],
        out_specs=[row(wq.shape[1]), row(wkv.shape[1]), row(wg.shape[1]), row(wa.shape[1])],
        out_shape=[jax.ShapeDtypeStruct((T, wq.shape[1]), BF16),
                   jax.ShapeDtypeStruct((T, wkv.shape[1]), BF16),
                   jax.ShapeDtypeStruct((T, wg.shape[1]), F32),
                   jax.ShapeDtypeStruct((T, wa.shape[1]), F32)],
        compiler_params=_params(("parallel",)),
        name="in_proj",
    )(x2d, g, b, wq, wkv, wg, wa, wgt)


def _compress_kernel(ch_ref, pe_ref, w1a_ref, w1b_ref, w1_ref, w2_ref, tail_ref, out_ref, *, n_rows):
    ch = ch_ref[0, 0]
    first = _dot(ch, w1a_ref[...])
    second = _dot(ch, w1b_ref[...])
    second = pltpu.roll(second, n_rows - 1, 0)
    pe_term = _dot(pe_ref[...], w1_ref[...])[0:1, :]
    hid = jax.nn.gelu(first + second + pe_term).astype(BF16)
    res = _dot(hid, w2_ref[...]) + tail_ref[0:1, :]
    out_ref[0, 0, 0:CMP_FRAME, :] = jnp.broadcast_to(tail_ref[1:2, :], (CMP_FRAME, LANES))
    out_ref[0, 0, CMP_FRAME:CMP_FRAME + n_rows, :] = res


def _compress(ch, pe8, w1a, w1b, w1, w2p, tail):
    B, Hk, n_rows, width = ch.shape
    kern = functools.partial(_compress_kernel, n_rows=n_rows)
    return pl.pallas_call(
        kern,
        grid=(B, Hk),
        in_specs=[pl.BlockSpec((1, 1, n_rows, width), lambda b, h: (b, h, 0, 0)),
                  _const_spec(pe8.shape), _const_spec(w1a.shape), _const_spec(w1b.shape),
                  _const_spec(w1.shape), _const_spec(w2p.shape), _const_spec(tail.shape)],
        out_specs=pl.BlockSpec((1, 1, CMP_FRAME + n_rows, LANES), lambda b, h: (b, h, 0, 0)),
        out_shape=jax.ShapeDtypeStruct((B, Hk, CMP_FRAME + n_rows, LANES), F32),
        compiler_params=_params(("parallel", "parallel")),
        name="compress",
    )(ch, pe8, w1a, w1b, w1, w2p, tail)


def _softmax_update(carry, s, v):
    m, acc = carry
    m_new = jnp.maximum(m, jnp.max(s, axis=-1, keepdims=True))
    p = jnp.exp2(s - m_new).astype(BF16)
    acc = acc * jnp.exp2(m - m_new) + _dot(p, v)
    return m_new, acc


def _softmax_merge(a, b):
    m = jnp.maximum(a[0], b[0])
    return m, a[1] * jnp.exp2(a[0] - m) + b[1] * jnp.exp2(b[0] - m)


def _add_on_lanes(s, lo, tile):
    hi = lo + tile.shape[1]
    parts = [s[:, :lo]] if lo else []
    parts.append(s[:, lo:hi] + tile)
    if hi < s.shape[1]:
        parts.append(s[:, hi:])
    return jnp.concatenate(parts, axis=1)


def _inv_row_sum(acc):
    return 1.0 / jnp.maximum(acc[:, COL_ONE:COL_ONE + 1], 1e-30)


def _nsa_kernel(q_ref, gate_ref, kc_ref, vc_ref, ks_ref, vs_ref, kw_ref, vw_ref,
                eye_ref, ovt_ref, wt_ref, nt_ref, ct_ref, qx_ref, pick_ref, place_ref, out_ref):
    si = pl.program_id(2)
    NB, G, QB = STEP_BLOCKS, NSA_GROUP, Q_BLOCK
    rows = NB * G * QB
    groups = [(b, g) for b in range(NB) for g in range(G)]
    row_slice = lambda b, g: slice((b * G + g) * QB, (b * G + g + 1) * QB)
    init = (jnp.full((rows, 1), M_FLOOR, F32), jnp.zeros((rows, LANES), F32))

    qblk = q_ref[0]
    q_pad = jnp.concatenate([_dot(qblk[b * QB:(b + 1) * QB], pick_ref[g]) for b, g in groups], axis=0)
    q_pad = (q_pad + qx_ref[0]).astype(BF16)
    q0 = si * STEP_Q

    c0 = pl.multiple_of(SUBLANES * NB * (si + 1), SUBLANES)
    kc = kc_ref[0, 0, pl.ds(c0, CMP_FRAME), :].astype(BF16)
    vc = vc_ref[0, 0, pl.ds(c0, CMP_FRAME), :].astype(BF16)
    s_c = _add_on_lanes(_dot_nt(q_pad, kc), CMP_FRAME - LANES, ct_ref[0])
    m_c = jnp.maximum(jnp.max(s_c, axis=-1, keepdims=True), M_FLOOR)
    e_c = jnp.exp2(s_c - m_c)
    p_c = e_c * (1.0 / jnp.maximum(jnp.sum(e_c, axis=-1, keepdims=True), 1e-30))
    o_c = _dot(p_c.astype(BF16), vc)

    p_sum = jnp.concatenate([sum(p_c[row_slice(b, g)] for g in range(G)) for b in range(NB)], axis=0)
    p_hi = p_sum.astype(BF16)
    p_lo = (p_sum - p_hi.astype(F32)).astype(BF16)
    imp_t = _dot_nt(ovt_ref[...], p_hi) + _dot_nt(ovt_ref[...], p_lo)
    jr = lax.broadcasted_iota(jnp.int32, (SEL_FRAME, STEP_Q), 0)
    ql = lax.broadcasted_iota(jnp.int32, (SEL_FRAME, STEP_Q), 1)
    j_abs = jr + (2 * NB * (si + 1) - SEL_FRAME)
    cur = (SEL_FRAME - 2 * NB) + ql // SEL_BLOCK
    exists = j_abs >= 0
    forced = exists & ((j_abs == 0) | (jr == cur) | (jr == cur - 1))
    valid = exists & (jr <= cur)
    free = valid & jnp.logical_not(forced)
    big = 3.0e38
    score = jnp.where(free, imp_t, -big)
    chosen = jnp.zeros((SEL_FRAME, STEP_Q), jnp.bool_)
    for _ in range(SEL_TOPK - 3):
        top = jnp.max(score, axis=0, keepdims=True)
        first = jnp.min(jnp.where(score == top, jr, SEL_FRAME), axis=0, keepdims=True)
        hit = jr == first
        chosen = chosen | hit
        score = jnp.where(hit, -big, score)
    neg_t = jnp.where(forced | (chosen & free), 0.0, NEG)
    neg = neg_t.T.astype(BF16)
    neg_rows = jnp.concatenate([neg[b * QB:(b + 1) * QB] for b, _ in groups], axis=0)
    q_sel = jnp.concatenate([neg_rows, q_pad], axis=1)

    r_win = q0 - WIN + KEY_PAD
    carry_w = init
    for lo in range(0, WIN + STEP_Q, FAR_CHUNK):
        n = min(FAR_CHUNK, WIN + STEP_Q - lo)
        r = pl.multiple_of(r_win + lo, LANES)
        s_w = _dot_nt(q_pad, kw_ref[0, 0, pl.ds(r, n), :]) + wt_ref[0, :, lo:lo + n]
        carry_w = _softmax_update(carry_w, s_w, vw_ref[0, 0, pl.ds(r, n), :])
    acc_w = carry_w[1]

    near_blocks = 2 * STEP_Q // SEL_BLOCK
    far_end = q0 - STEP_Q + KEY_PAD
    n_far = (jnp.maximum(q0 - STEP_Q, 0) + FAR_CHUNK - 1) // FAR_CHUNK

    def far_chunk(chain, i, live):
        r0 = pl.multiple_of(jnp.where(live, far_end - FAR_CHUNK * (i + 1), 0), LANES)
        e0 = jnp.where(live, SEL_BLOCK * (SEL_FRAME - near_blocks) - FAR_CHUNK * (i + 1) + KEY_PAD, 0)
        e0 = pl.multiple_of(e0, LANES)
        keys = jnp.concatenate([eye_ref[pl.ds(e0, FAR_CHUNK), :], ks_ref[0, 0, pl.ds(r0, FAR_CHUNK), :]], axis=1)
        s = _dot_nt(q_sel, keys)
        return _softmax_update(chain, s, vs_ref[0, 0, pl.ds(r0, FAR_CHUNK), :])

    def far_pair(i, carry):
        even, odd = carry
        return far_chunk(even, 2 * i, True), far_chunk(odd, 2 * i + 1, 2 * i + 1 < n_far)

    even, odd = lax.fori_loop(0, (n_far + 1) // 2, far_pair, (init, init))
    carry = _softmax_merge(even, odd)
    r_near = pl.multiple_of(far_end, LANES)
    e_near = SEL_BLOCK * (SEL_FRAME - near_blocks) + KEY_PAD
    keys = jnp.concatenate([eye_ref[e_near:e_near + 2 * STEP_Q, :], ks_ref[0, 0, pl.ds(r_near, 2 * STEP_Q), :]],
                           axis=1)
    s_n = _dot_nt(q_sel, keys) + nt_ref[0]
    _, acc_s = _softmax_update(carry, s_n, vs_ref[0, 0, pl.ds(r_near, 2 * STEP_Q), :])

    gates = gate_ref[0]
    def gate_col(br):
        return jnp.concatenate([gates[b * QB:(b + 1) * QB, 3 * g + br:3 * g + br + 1] for b, g in groups], axis=0)
    o = (gate_col(0) * o_c + (gate_col(1) * _inv_row_sum(acc_s)) * acc_s
         + (gate_col(2) * _inv_row_sum(acc_w)) * acc_w).astype(BF16)
    out = [sum(_dot(o[row_slice(b, g)], place_ref[g]) for g in range(G)) for b in range(NB)]
    out_ref[0] = jnp.concatenate(out, axis=0).astype(BF16)


def _nsa(q, gates, kc, vc, ks, vs, kw, vw, eye, ovt, wt, nt, ct, qx, pick, place):
    B, S, _ = q.shape
    Hk = NSA_KV_HEADS
    per_head = lambda a: pl.BlockSpec((1, 1) + a.shape[2:], lambda b, h, i: (b, h, 0, 0))
    tab = lambda a: pl.BlockSpec((1,) + a.shape[1:], lambda b, h, i: (h, 0, 0))
    qspec = pl.BlockSpec((1, STEP_Q, NSA_GROUP * HEAD_DIM), lambda b, h, i: (b, i, h))
    return pl.pallas_call(
        _nsa_kernel,
        grid=(B, Hk, S // STEP_Q),
        in_specs=[qspec, pl.BlockSpec((1, STEP_Q, LANES), lambda b, h, i: (b, i, h)),
                  per_head(kc), per_head(vc), per_head(ks), per_head(vs), per_head(kw), per_head(vw),
                  _const_spec(eye.shape), _const_spec(ovt.shape), tab(wt), tab(nt), tab(ct), tab(qx),
                  _const_spec(pick.shape), _const_spec(place.shape)],
        out_specs=qspec,
        out_shape=jax.ShapeDtypeStruct(q.shape, BF16),
        compiler_params=_params(("parallel", "parallel", "arbitrary")),
        name="nsa",
    )(q, gates, kc, vc, ks, vs, kw, vw, eye, ovt, wt, nt, ct, qx, pick, place)


CONV_HALO = 32


def _conv_kernel(cur_ref, prev_ref, w_ref, b_ref, g_ref, beta_ref, out_ref):
    i = pl.program_id(1)
    cur = cur_ref[0]
    rows = cur.shape[0]
    halo = prev_ref[0, rows - CONV_HALO:rows, :]
    halo = jnp.where(i > 0, halo, 0.0)
    ext = jnp.concatenate([halo, cur], axis=0)
    acc = b_ref[...]
    for s in range(SUBLANES):
        part = 0.0
        for a in range((CONV_K - 1 - s) // SUBLANES + 1):
            lo = CONV_HALO - SUBLANES * (a + 1)
            k = CONV_K - 1 - SUBLANES * a - s
            part = part + ext[lo:lo + rows + SUBLANES, :] * w_ref[k:k + 1, :]
        acc = acc + part[SUBLANES - s:SUBLANES - s + rows, :]
    y = _ln(acc, g_ref[...], beta_ref[...])
    out_ref[0] = (y * jax.nn.sigmoid(y)).astype(BF16)


def _conv(hglu, w, b, g, beta):
    B, S, C = hglu.shape
    rows = ROW_TILE
    return pl.pallas_call(
        _conv_kernel,
        grid=(B, S // rows),
        in_specs=[pl.BlockSpec((1, rows, C), lambda b, i: (b, i, 0)),
                  pl.BlockSpec((1, rows, C), lambda b, i: (b, jnp.maximum(i - 1, 0), 0)),
                  _const_spec(w.shape), _const_spec(b.shape), _const_spec(g.shape), _const_spec(beta.shape)],
        out_specs=pl.BlockSpec((1, rows, C), lambda b, i: (b, i, 0)),
        out_shape=jax.ShapeDtypeStruct((B, S, C), BF16),
        compiler_params=_params(("parallel", "arbitrary")),
        name="conv",
    )(hglu, hglu, w, b, g, beta)


def _out_proj_kernel(nsa_ref, conv_ref, x_ref, gin_ref, bin_ref, wa_ref, wb_ref, g_ref, b_ref, out_ref):
    h = _ln(x_ref[...], gin_ref[...], bin_ref[...])
    mix = _dot(nsa_ref[...], wa_ref[...]) + _dot(conv_ref[...], wb_ref[...])
    out_ref[...] = _ln(ALPHA * h + mix, g_ref[...], b_ref[...])


def _out_proj(o_nsa, o_conv, x2d, gin, bin_, wa, wb, g, b):
    T, D = x2d.shape
    tm = ROW_TILE
    row = lambda n: pl.BlockSpec((tm, n), lambda i: (i, 0))
    vec = _const_spec((1, D))
    return pl.pallas_call(
        _out_proj_kernel,
        grid=(T // tm,),
        in_specs=[row(o_nsa.shape[1]), row(o_conv.shape[1]), row(D), vec, vec,
                  _const_spec(wa.shape), _const_spec(wb.shape), vec, vec],
        out_specs=row(D),
        out_shape=jax.ShapeDtypeStruct((T, D), F32),
        compiler_params=_params(("parallel",)),
        name="out_proj",
    )(o_nsa, o_conv, x2d, gin, bin_, wa, wb, g, b)


def _mem_kv_kernel(m_ref, g_ref, b_ref, w_ref, out_ref):
    out_ref[...] = _dot(_ln(m_ref[...], g_ref[...], b_ref[...]).astype(BF16), w_ref[...]).astype(BF16)


def _mem_kv(mem2d, g, b, wkv):
    R, D = mem2d.shape
    N = wkv.shape[1]
    tn = 1024
    return pl.pallas_call(
        _mem_kv_kernel,
        grid=(R // ROW_TILE, N // tn),
        in_specs=[pl.BlockSpec((ROW_TILE, D), lambda i, j: (i, 0)), _const_spec((1, D)), _const_spec((1, D)),
                  pl.BlockSpec((D, tn), lambda i, j: (0, j))],
        out_specs=pl.BlockSpec((ROW_TILE, tn), lambda i, j: (i, j)),
        out_shape=jax.ShapeDtypeStruct((R, N), BF16),
        compiler_params=_params(("parallel", "parallel")),
        name="mem_kv",
    )(mem2d, g, b, wkv)


def _xattn_kernel(x_ref, kv_ref, wq_ref, wo_ref, g_ref, b_ref, rw_ref, rw_hi_ref, rb_ref,
                  x2_ref, x2p_ref, idx_ref, wt_ref):
    D = x_ref.shape[1]
    dh = D // X_HEADS
    x1 = x_ref[...]
    q = _dot(x1.astype(BF16), wq_ref[...]).astype(BF16)
    heads = []
    for h in range(X_HEADS):
        k = kv_ref[0, :, h * dh:(h + 1) * dh]
        v = kv_ref[0, :, D + h * dh:D + (h + 1) * dh]
        s = _dot_nt(q[:, h * dh:(h + 1) * dh], k) * (dh ** -0.5)
        e = jnp.exp(s - jnp.max(s, axis=-1, keepdims=True))
        p = e / jnp.sum(e, axis=-1, keepdims=True)
        heads.append(_dot(p.astype(BF16), v).astype(BF16))
    xa = _dot(jnp.concatenate(heads, axis=1), wo_ref[...])
    x2 = _ln(ALPHA * x1 + xa, g_ref[...], b_ref[...])
    x2_ref[...] = x2
    _rows_to_tiles(x2p_ref, _pack_halves(x2))

    x_hi = x2.astype(BF16)
    x_lo = (x2 - x_hi.astype(F32)).astype(BF16)
    both = _dot(x_hi, rw_ref[...])
    logit = both[:, 0:LANES] + both[:, LANES:2 * LANES] + _dot(x_lo, rw_hi_ref[...]) + rb_ref[...]
    lane = lax.broadcasted_iota(jnp.int32, logit.shape, 1)
    big = 3.0e38
    lg = jnp.where(lane < MOE_GROUPS, logit, -big)
    lg_max = jnp.max(lg, axis=-1, keepdims=True)
    grp = jnp.min(jnp.where(lg == lg_max, lane, LANES), axis=-1, keepdims=True)
    w_grp = 1.0 / jnp.sum(jnp.exp(lg - lg_max), axis=-1, keepdims=True)
    e_lane = lane - MOE_GROUPS
    in_grp = (e_lane >= grp * MOE_EPG) & (e_lane < (grp + 1) * MOE_EPG)
    le = jnp.where(in_grp, logit, -big)
    v1 = jnp.max(le, axis=-1, keepdims=True)
    i1 = jnp.min(jnp.where(le == v1, lane, LANES), axis=-1, keepdims=True)
    le2 = jnp.where(lane == i1, -big, le)
    v2 = jnp.max(le2, axis=-1, keepdims=True)
    i2 = jnp.min(jnp.where(le2 == v2, lane, LANES), axis=-1, keepdims=True)
    r = jnp.exp(v2 - v1)
    w1 = w_grp / (1.0 + r)
    w2 = w_grp * r / (1.0 + r)
    idx_ref[...] = jnp.where(lane == 0, i1 - MOE_GROUPS, jnp.where(lane == 1, i2 - MOE_GROUPS, 0))
    wt_ref[...] = jnp.where(lane == 0, w1, jnp.where(lane == 1, w2, 0.0))


def _xattn(x1, mem_kv, S, wq, wo, g, b, rw, rw_hi, rb):
    T, D = x1.shape
    tm = ROW_TILE
    row = lambda n: pl.BlockSpec((tm, n), lambda i: (i, 0))
    vec = _const_spec((1, D))
    per_step = S // tm
    return pl.pallas_call(
        _xattn_kernel,
        grid=(T // tm,),
        in_specs=[row(D), pl.BlockSpec((1,) + mem_kv.shape[1:], lambda i: (i // per_step, 0, 0)),
                  _const_spec(wq.shape), _const_spec(wo.shape), vec, vec,
                  _const_spec(rw.shape), _const_spec(rw_hi.shape), _const_spec(rb.shape)],
        out_specs=[row(D), pl.BlockSpec((tm * SUBLANES, LANES), lambda i: (i, 0)), row(LANES), row(LANES)],
        out_shape=[jax.ShapeDtypeStruct((T, D), F32), jax.ShapeDtypeStruct((T * SUBLANES, LANES), jnp.int32),
                   jax.ShapeDtypeStruct((T, LANES), jnp.int32), jax.ShapeDtypeStruct((T, LANES), F32)],
        compiler_params=_params(("parallel",)),
        name="xattn",
    )(x1, mem_kv, wq, wo, g, b, rw, rw_hi, rb)


def _tile(ref, t):
    return ref.at[pl.ds(pl.multiple_of(t * SUBLANES, SUBLANES), SUBLANES), :]


def _gather_copy(x_hbm, a, xbuf, sem, r):
    tok = jnp.maximum(a, 0) >> 1
    return pltpu.make_async_copy(_tile(x_hbm, tok), _tile(xbuf, r), sem)


def _scatter_copy(ybuf, a, y_hbm, sem, r, n_real, slot):
    dst = jnp.where(a >= 0, a, n_real + slot * MOE_ROWS + r)
    return pltpu.make_async_copy(_tile(ybuf, r), _tile(y_hbm, dst), sem)


def _moe_ffn_kernel(blk_e_ref, prev_ref, cur_ref, nxt_ref, x_hbm, wg_ref, wu_ref, wd_ref, y_hbm,
                    xbuf, ybuf, gsem, ssem):
    i = pl.program_id(0)
    last = pl.num_programs(0) - 1
    slot = i % 2
    other = 1 - slot
    R = MOE_ROWS
    n_real = 2 * (x_hbm.shape[0] // SUBLANES)
    rows = range(R)
    gather = lambda ref, s, r: _gather_copy(x_hbm, ref[0, 0, r], xbuf.at[s], gsem.at[s], r)
    scatter = lambda a, s, r: _scatter_copy(ybuf.at[s], a, y_hbm, ssem.at[s], r, n_real, s)
    gather_wait = lambda s: pltpu.make_async_copy(xbuf.at[s], xbuf.at[s], gsem.at[s]).wait()
    scatter_wait = lambda s: pltpu.make_async_copy(ybuf.at[s], ybuf.at[s], ssem.at[s]).wait()

    @pl.when(i == 0)
    def _():
        ybuf[...] = jnp.zeros_like(ybuf)
        for r in rows:
            gather(cur_ref, 0, r).start()
            scatter(-1, 0, r).start()

    gather_wait(slot)
    scatter_wait(slot)
    for r in rows:
        gather(nxt_ref, other, r).start()
        scatter(jnp.where(i > 0, prev_ref[0, 0, r], -1), other, r).start()

    xb = _unpack_halves(_tiles_to_rows(xbuf.at[slot], R))
    hid = jax.nn.silu(_dot(xb, wg_ref[0])) * _dot(xb, wu_ref[0])
    _rows_to_tiles(ybuf.at[slot], _pack_halves(_dot(hid.astype(BF16), wd_ref[0])))

    @pl.when(i == last)
    def _():
        for r in rows:
            scatter(cur_ref[0, 0, r], slot, r).start()
        gather_wait(other)
        scatter_wait(other)
        scatter_wait(slot)


def _moe_ffn(a_buf, blk_e, x2p, w_gate, w_up, w_down):
    T = x2p.shape[0] // SUBLANES
    D = 2 * SUBLANES * LANES
    R = MOE_ROWS
    n_blk = a_buf.shape[0] // R
    Hd = w_gate.shape[2]
    a3 = a_buf.reshape(n_blk, 1, R)
    idx = lambda f: pl.BlockSpec((1, 1, R), lambda i, be: (f(i), 0, 0), memory_space=pltpu.SMEM)
    grid_spec = pltpu.PrefetchScalarGridSpec(
        num_scalar_prefetch=1,
        grid=(n_blk,),
        in_specs=[idx(lambda i: jnp.maximum(i - 1, 0)), idx(lambda i: i), idx(lambda i: jnp.minimum(i + 1, n_blk - 1)),
                  pl.BlockSpec(memory_space=pl.ANY),
                  pl.BlockSpec((1, D, Hd), lambda i, be: (be[i], 0, 0)),
                  pl.BlockSpec((1, D, Hd), lambda i, be: (be[i], 0, 0)),
                  pl.BlockSpec((1, Hd, D), lambda i, be: (be[i], 0, 0))],
        out_specs=pl.BlockSpec(memory_space=pl.ANY),
        scratch_shapes=[pltpu.VMEM((2, R * SUBLANES, LANES), jnp.int32), pltpu.VMEM((2, R * SUBLANES, LANES), jnp.int32),
                        pltpu.SemaphoreType.DMA((2,)), pltpu.SemaphoreType.DMA((2,))],
    )
    return pl.pallas_call(
        _moe_ffn_kernel,
        grid_spec=grid_spec,
        out_shape=jax.ShapeDtypeStruct(((2 * T + 2 * R) * SUBLANES, LANES), jnp.int32),
        compiler_params=_params(("arbitrary",)),
        name="moe_ffn",
    )(blk_e, a3, a3, a3, x2p, w_gate, w_up, w_down)


def _combine_kernel(y_ref, wt_ref, x_ref, g_ref, b_ref, out_ref):
    rows = x_ref.shape[0]
    wt = wt_ref[...]
    y_slot = lambda k: _unpack_halves(_tiles_to_rows(y_ref, rows, k * SUBLANES, 2 * SUBLANES)).astype(F32)
    moe = wt[:, 0:1] * y_slot(0) + wt[:, 1:2] * y_slot(1)
    out_ref[...] = _ln(ALPHA * x_ref[...] + moe, g_ref[...], b_ref[...])


def _combine(y, wts, x2, g, b):
    T, D = x2.shape
    tm = ROW_TILE
    row = lambda n: pl.BlockSpec((tm, n), lambda i: (i, 0))
    return pl.pallas_call(
        _combine_kernel,
        grid=(T // tm,),
        in_specs=[pl.BlockSpec((2 * tm * SUBLANES, LANES), lambda i: (i, 0)), row(LANES), row(D),
                  _const_spec((1, D)), _const_spec((1, D))],
        out_specs=row(D),
        out_shape=jax.ShapeDtypeStruct((T, D), F32),
        compiler_params=_params(("parallel",)),
        name="moe_combine",
    )(y, wts, x2, g, b)


def _t5_bucket(dist):
    n = np.maximum(dist, 0)
    max_exact = REL_BUCKETS // 2
    nf = np.maximum(n, 1).astype(np.float64)
    large = max_exact + (np.log(nf / max_exact) / math.log(REL_MAX_DIST / max_exact)
                         * (REL_BUCKETS - max_exact)).astype(np.int64)
    large = np.minimum(large, REL_BUCKETS - 1)
    return np.where(n < max_exact, n, large)


def _bias_tiles(rel_table):
    Hk, G, QB, NB = NSA_KV_HEADS, NSA_GROUP, Q_BLOCK, STEP_BLOCKS
    tab = rel_table.reshape(REL_BUCKETS, Hk, G).transpose(1, 2, 0) * LOG2E
    far = tab[:, :, REL_BUCKETS - 1]
    far_hi = far.astype(BF16).astype(F32)
    far_lo = far - far_hi
    rel = tab - far[:, :, None]

    def sub_tile(dist, valid):
        bucket = _t5_bucket(dist)
        if not (valid & (bucket != REL_BUCKETS - 1)).any():
            return jnp.broadcast_to(jnp.asarray(np.where(valid, 0.0, NEG), F32), (Hk, G) + dist.shape)
        onehot = jnp.asarray(bucket[None] == np.arange(REL_BUCKETS)[:, None, None], F32)
        bias = jnp.einsum('hgk,kqn->hgqn', rel, onehot, precision=lax.Precision.HIGHEST)
        return jnp.where(jnp.asarray(valid), bias, NEG)

    def tile(dist_fn, valid_fn, n_keys):
        qr = np.arange(QB)[:, None]
        kk = np.arange(LANES)[None, :]
        blocks = []
        for b in range(NB):
            subs = []
            for c in range(n_keys // LANES):
                dist = dist_fn(b, qr, kk + c * LANES)
                subs.append(sub_tile(dist, valid_fn(dist)))
            blocks.append(jnp.concatenate(subs, axis=-1))
        return jnp.stack(blocks, axis=1).reshape(Hk, NB * G * QB, n_keys).astype(F32)

    wt = tile(lambda b, qr, kr: WIN + QB * b + qr - kr, lambda d: (d >= 0) & (d < WIN), WIN + STEP_Q)
    nt = tile(lambda b, qr, kr: STEP_Q + QB * b + qr - kr, lambda d: d >= 0, 2 * STEP_Q)
    frame_lo = CMP_FRAME - LANES
    ct = tile(lambda b, qr, r: (QB * (b - NB) + qr + CMP_STRIDE * (CMP_FRAME - frame_lo - r) - (CMP_BLOCK - 1)),
              lambda d: d >= 0, LANES)
    lane = np.arange(LANES)[None, None, None, :]
    qx = (jnp.where(lane == COL_ONE, far_hi[:, :, None, None], 0.0)
          + jnp.where(lane == COL_ONE2, far_lo[:, :, None, None], 0.0)
          + jnp.where(lane == COL_PAD, NEG, 0.0))
    qx = jnp.broadcast_to(qx[:, None], (Hk, NB, G, QB, LANES)).reshape(Hk, NB * G * QB, LANES)
    return wt, nt, ct, qx.astype(F32)


def _static_tables():
    G = NSA_GROUP
    rows = np.arange(SEL_FRAME * SEL_BLOCK)
    eye = (rows[:, None] // SEL_BLOCK == np.arange(SEL_FRAME)[None, :]).astype(np.float32)
    eye = np.concatenate([np.zeros((KEY_PAD, SEL_FRAME), np.float32), eye], axis=0)
    cs = np.arange(CMP_FRAME) * CMP_STRIDE
    ss = np.arange(SEL_FRAME) * SEL_BLOCK
    ov = np.clip(np.minimum(cs[:, None] + CMP_BLOCK, ss[None, :] + SEL_BLOCK)
                 - np.maximum(cs[:, None], ss[None, :]), 0, None) / CMP_BLOCK
    pick = np.zeros((G, G * HEAD_DIM, LANES), np.float32)
    place = np.zeros((G, LANES, G * HEAD_DIM), np.float32)
    for g in range(G):
        for d in range(HEAD_DIM):
            pick[g, g * HEAD_DIM + d, d] = 1.0
            place[g, d, g * HEAD_DIM + d] = 1.0
    return (jnp.asarray(eye, BF16), jnp.asarray(ov.T, BF16), jnp.asarray(pick, BF16), jnp.asarray(place, BF16))


def _key_layout(t, B, S, ones_cols, pad_col):
    Hk = NSA_KV_HEADS
    t = t.reshape(B, S, Hk, HEAD_DIM).transpose(0, 2, 1, 3)
    lane = np.arange(LANES - HEAD_DIM) + HEAD_DIM
    extra = np.isin(lane, ones_cols).astype(np.float32)
    body = jnp.concatenate([t, jnp.broadcast_to(jnp.asarray(extra, t.dtype), (B, Hk, S, LANES - HEAD_DIM))], axis=-1)
    pad_row = np.zeros((LANES,), np.float32)
    if pad_col is not None:
        pad_row[pad_col] = 1.0
    pad = jnp.broadcast_to(jnp.asarray(pad_row, t.dtype), (B, Hk, KEY_PAD, LANES))
    return jnp.concatenate([pad, body], axis=2)


def _dispatch(idx, T):
    R, E = MOE_ROWS, MOE_EXPERTS
    A = 2 * T
    id_bits = max(A - 1, 1).bit_length()
    e_flat = idx[:, 0:2].reshape(A)
    counts = jnp.sum((e_flat[:, None] == jnp.arange(E, dtype=jnp.int32)[None, :]).astype(jnp.int32), axis=0)
    n_pad = (-counts) % R
    real = (e_flat << (id_bits + 1)) | jnp.arange(A, dtype=jnp.int32)
    d_e = jnp.arange(E, dtype=jnp.int32)[:, None]
    d_r = jnp.arange(R, dtype=jnp.int32)[None, :]
    pad = jnp.where(d_r < n_pad[:, None], (d_e << (id_bits + 1)) | (1 << id_bits) | d_r,
                    (E << (id_bits + 1)) | (d_e * R + d_r))
    keys = jnp.sort(jnp.concatenate([real, pad.reshape(E * R)]))
    expert = keys >> (id_bits + 1)
    is_pad = ((keys >> id_bits) & 1) == 1
    a_buf = jnp.where(is_pad | (expert >= E), -1, keys & ((1 << id_bits) - 1))
    blk_e = jnp.minimum(expert.reshape(-1, R)[:, 0], E - 1)
    return a_buf, blk_e


def _forward(x, mem, ln_in_g, ln_in_b, w_in, cmp_pe_k, cmp_w1_k, cmp_w2_k, cmp_pe_v, cmp_w1_v, cmp_w2_v, rel_table, conv_dw_w, conv_dw_b, conv_ln_g, conv_ln_b, w_out, ln1_g, ln1_b, mem_ln_g, mem_ln_b, xa_wq, xa_wkv, xa_wo, ln2_g, ln2_b, router_group_w, router_group_b, router_expert_w, router_expert_b, moe_w_gate, moe_w_up, moe_w_down, ln3_g, ln3_b):
    B, S, D = x.shape
    T = B * S
    Hk, G, dh = NSA_KV_HEADS, NSA_GROUP, HEAD_DIM
    assert S % ROW_TILE == 0 and S % STEP_Q == 0 and SEL_TOPK * SEL_BLOCK <= S <= SEL_FRAME * SEL_BLOCK
    vec = lambda v: v.reshape(1, -1).astype(F32)
    x2d = x.reshape(T, D)

    w = w_in[0]
    nq, nkv = NSA_HEADS * dh, Hk * dh
    c_g = nq + 6 * nkv
    c_u = c_g + NSA_HEADS * 3
    cw = (w.shape[1] - c_u) // 2
    wq = w[:, :nq].astype(BF16)
    wkv = w[:, nq:c_g].astype(BF16)
    wg = jnp.pad(w[:, c_g:c_u].reshape(D, Hk, G * 3), ((0, 0), (0, 0), (0, LANES - G * 3)))
    wg = wg.reshape(D, Hk * LANES).astype(BF16)
    wa = w[:, c_u:c_u + cw].astype(BF16)
    wgt = w[:, c_u + cw:].astype(BF16)
    q, kv, gates, hglu = _in_proj(x2d, vec(ln_in_g), vec(ln_in_b), wq, wkv, wg, wa, wgt)

    def compressed(col, pe, w1, w2, ones_cols, pad_col):
        t = kv[:, col * nkv:(col + 1) * nkv].reshape(B, S, Hk, dh).transpose(0, 2, 1, 3)
        ch = t.reshape(B, Hk, S // CMP_STRIDE, CMP_STRIDE * dh)
        half = CMP_STRIDE * dh
        pe8 = jnp.pad(pe.reshape(1, CMP_BLOCK * dh), ((0, 7), (0, 0))).astype(BF16)
        w2p = jnp.pad(w2, ((0, 0), (0, LANES - dh))).astype(BF16)
        lane = np.arange(LANES)
        tail = np.zeros((8, LANES), np.float32)
        tail[0] = np.isin(lane, ones_cols)
        if pad_col is not None:
            tail[1, pad_col] = 1.0
        w1b16 = w1.astype(BF16)
        return _compress(ch, pe8, w1b16[:half], w1b16[half:], w1b16, w2p, jnp.asarray(tail))

    kc = compressed(0, cmp_pe_k[0], cmp_w1_k[0], cmp_w2_k[0], (COL_ONE, COL_ONE2), COL_PAD)
    vc = compressed(1, cmp_pe_v[0], cmp_w1_v[0], cmp_w2_v[0], (), None)

    part = lambda c: kv[:, c * nkv:(c + 1) * nkv]
    ks = _key_layout(part(2), B, S, (COL_ONE, COL_ONE2), COL_PAD)
    vs = _key_layout(part(3), B, S, (COL_ONE,), None)
    kw = _key_layout(part(4), B, S, (COL_ONE, COL_ONE2), COL_PAD)
    vw = _key_layout(part(5), B, S, (COL_ONE,), None)
    eye, ovt, pick, place = _static_tables()
    wt, nt, ct, qx = _bias_tiles(rel_table.astype(F32))
    o_nsa = _nsa(q.reshape(B, S, nq), gates.reshape(B, S, Hk * LANES), kc, vc, ks, vs, kw, vw,
                 eye, ovt, wt, nt, ct, qx, pick, place).reshape(T, nq)

    o_conv = _conv(hglu.reshape(B, S, cw), conv_dw_w[0].reshape(CONV_K, cw), vec(conv_dw_b[0]),
                   vec(conv_ln_g[0]), vec(conv_ln_b[0])).reshape(T, cw)

    wo = w_out[0].astype(BF16)
    x1 = _out_proj(o_nsa, o_conv, x2d, vec(ln_in_g), vec(ln_in_b), wo[:nq], wo[nq:], vec(ln1_g[0]), vec(ln1_b[0]))

    mkv = _mem_kv(mem.reshape(-1, D), vec(mem_ln_g[0]), vec(mem_ln_b[0]), xa_wkv[0].astype(BF16))
    mkv = mkv.reshape(B, mem.shape[1], 2 * D)
    rw = jnp.concatenate([router_group_w[0], router_expert_w[0]], axis=1).astype(F32)
    rw = jnp.pad(rw, ((0, 0), (0, LANES - rw.shape[1])))
    rb = jnp.pad(jnp.concatenate([router_group_b[0], router_expert_b[0]]).astype(F32), (0, LANES - MOE_GROUPS - MOE_EXPERTS))
    rw_hi = rw.astype(BF16)
    rw_lo = (rw - rw_hi.astype(F32)).astype(BF16)
    x2, x2p, idx, wts = _xattn(x1, mkv, S, xa_wq[0].astype(BF16), xa_wo[0].astype(BF16), vec(ln2_g[0]), vec(ln2_b[0]),
                          jnp.concatenate([rw_hi, rw_lo], axis=1), rw_hi, rb.reshape(1, LANES))

    a_buf, blk_e = _dispatch(idx, T)
    y = _moe_ffn(a_buf, blk_e, x2p, moe_w_gate[0].astype(BF16), moe_w_up[0].astype(BF16),
                 moe_w_down[0].astype(BF16))
    out = _combine(y, wts, x2, vec(ln3_g[0]), vec(ln3_b[0]))
    stages = dict(q=q, kc=kc, vc=vc, o_nsa=o_nsa, o_conv=o_conv, x1=x1, x2=x2)
    return out.reshape(B, S, D), stages


def kernel(x, mem, ln_in_g, ln_in_b, w_in, cmp_pe_k, cmp_w1_k, cmp_w2_k, cmp_pe_v, cmp_w1_v, cmp_w2_v, rel_table, conv_dw_w, conv_dw_b, conv_ln_g, conv_ln_b, w_out, ln1_g, ln1_b, mem_ln_g, mem_ln_b, xa_wq, xa_wkv, xa_wo, ln2_g, ln2_b, router_group_w, router_group_b, router_expert_w, router_expert_b, moe_w_gate, moe_w_up, moe_w_down, ln3_g, ln3_b):
    out, _ = _forward(x, mem, ln_in_g, ln_in_b, w_in, cmp_pe_k, cmp_w1_k, cmp_w2_k, cmp_pe_v, cmp_w1_v, cmp_w2_v, rel_table, conv_dw_w, conv_dw_b, conv_ln_g, conv_ln_b, w_out, ln1_g, ln1_b, mem_ln_g, mem_ln_b, xa_wq, xa_wkv, xa_wo, ln2_g, ln2_b, router_group_w, router_group_b, router_expert_w, router_expert_b, moe_w_gate, moe_w_up, moe_w_down, ln3_g, ln3_b)
    return out
```

```python
import functools
import math

import jax
import jax.numpy as jnp
import numpy as np
from jax import lax
from jax.experimental import pallas as pl
from jax.experimental.pallas import tpu as pltpu

HEAD_DIM = 64
NSA_HEADS = 16
NSA_KV_HEADS = 4
NSA_GROUP = 4
CMP_BLOCK = 32
CMP_STRIDE = 16
CMP_HIDDEN = 256
SEL_BLOCK = 64
SEL_TOPK = 16
WIN = 512
Q_BLOCK = 128
CONV_K = 31
REL_BUCKETS = 32
REL_MAX_DIST = 128
X_HEADS = 4
MOE_GROUPS = 4
MOE_EPG = 8
MOE_EXPERTS = 32
LN_EPS = 1e-5
DEPTH = 1
ALPHA = (2 * DEPTH) ** 0.25

LANES = 128
SUBLANES = 8
VMEM_LIMIT = 56 * 1024 * 1024
ROW_TILE = 256
MOE_ROWS = 256

STEP_BLOCKS = 2
STEP_Q = STEP_BLOCKS * Q_BLOCK
CMP_FRAME = 512
SEL_FRAME = 128
KEY_PAD = 512
FAR_CHUNK = 512
NEG = -float(2 ** 30)
M_FLOOR = -float(2 ** 27)
LOG2E = math.log2(math.e)
COL_ONE, COL_PAD, COL_ONE2 = 64, 65, 66

F32 = jnp.float32
BF16 = jnp.bfloat16


def _dot(a, b):
    return jnp.dot(a, b, preferred_element_type=F32)


def _dot_nt(a, b):
    return lax.dot_general(a, b, (((1,), (1,)), ((), ())), preferred_element_type=F32)


def _ln(x, g, b):
    mu = jnp.mean(x, axis=-1, keepdims=True)
    xc = x - mu
    var = jnp.mean(xc * xc, axis=-1, keepdims=True)
    return xc * lax.rsqrt(var + LN_EPS) * g + b


def _pack_halves(x):
    n = x.shape[1] // 2
    bits = lambda v: lax.bitcast_convert_type(v.astype(BF16).astype(F32), jnp.uint32)
    return lax.bitcast_convert_type((bits(x[:, :n]) >> 16) | bits(x[:, n:]), jnp.int32)


def _unpack_halves(w):
    u = lax.bitcast_convert_type(w, jnp.uint32)
    lo = lax.bitcast_convert_type(u << 16, F32).astype(BF16)
    hi = lax.bitcast_convert_type(u & jnp.uint32(0xFFFF0000), F32).astype(BF16)
    return jnp.concatenate([lo, hi], axis=1)


def _rows_to_tiles(ref, packed):
    rows = packed.shape[0]
    for s in range(SUBLANES):
        ref[pl.ds(s, rows, stride=SUBLANES), :] = packed[:, s * LANES:(s + 1) * LANES]


def _tiles_to_rows(ref, rows, first=0, step=SUBLANES):
    return jnp.concatenate([ref[pl.ds(first + s, rows, stride=step), :] for s in range(SUBLANES)], axis=1)


def _params(sem):
    return pltpu.CompilerParams(dimension_semantics=sem, vmem_limit_bytes=VMEM_LIMIT)


def _const_spec(shape):
    nd = len(shape)
    return pl.BlockSpec(shape, lambda *_: (0,) * nd)


def _in_proj_kernel(x_ref, g_ref, b_ref, wq_ref, wkv_ref, wg_ref, wa_ref, wgt_ref,
                    q_ref, kv_ref, gate_ref, glu_ref):
    h = _ln(x_ref[...], g_ref[...], b_ref[...]).astype(BF16)
    q_ref[...] = (_dot(h, wq_ref[...]) * (HEAD_DIM ** -0.5 * LOG2E)).astype(BF16)
    kv_ref[...] = _dot(h, wkv_ref[...]).astype(BF16)
    gate_ref[...] = jax.nn.sigmoid(_dot(h, wg_ref[...]))
    glu_ref[...] = _dot(h, wa_ref[...]) * jax.nn.sigmoid(_dot(h, wgt_ref[...]))


def _in_proj(x2d, g, b, wq, wkv, wg, wa, wgt):
    T, D = x2d.shape
    tm = ROW_TILE
    row = lambda n: pl.BlockSpec((tm, n), lambda i: (i, 0))
    return pl.pallas_call(
        _in_proj_kernel,
        grid=(T // tm,),
        in_specs=[row(D), _const_spec((1, D)), _const_spec((1, D)), _const_spec(wq.shape),
                  _const_spec(wkv.shape), _const_spec(wg.shape), _const_spec(wa.shape),
                  _const_spec(wgt.shape)],
        out_specs=[row(wq.shape[1]), row(wkv.shape[1]), row(wg.shape[1]), row(wa.shape[1])],
        out_shape=[jax.ShapeDtypeStruct((T, wq.shape[1]), BF16),
                   jax.ShapeDtypeStruct((T, wkv.shape[1]), BF16),
                   jax.ShapeDtypeStruct((T, wg.shape[1]), F32),
                   jax.ShapeDtypeStruct((T, wa.shape[1]), F32)],
        compiler_params=_params(("parallel",)),
        name="in_proj",
    )(x2d, g, b, wq, wkv, wg, wa, wgt)


def _compress_kernel(ch_ref, pe_ref, w1a_ref, w1b_ref, w1_ref, w2_ref, tail_ref, out_ref, *, n_rows):
    ch = ch_ref[0, 0]
    first = _dot(ch, w1a_ref[...])
    second = _dot(ch, w1b_ref[...])
    second = pltpu.roll(second, n_rows - 1, 0)
    pe_term = _dot(pe_ref[...], w1_ref[...])[0:1, :]
    hid = jax.nn.gelu(first + second + pe_term).astype(BF16)
    res = _dot(hid, w2_ref[...]) + tail_ref[0:1, :]
    out_ref[0, 0, 0:CMP_FRAME, :] = jnp.broadcast_to(tail_ref[1:2, :], (CMP_FRAME, LANES))
    out_ref[0, 0, CMP_FRAME:CMP_FRAME + n_rows, :] = res


def _compress(ch, pe8, w1a, w1b, w1, w2p, tail):
    B, Hk, n_rows, width = ch.shape
    kern = functools.partial(_compress_kernel, n_rows=n_rows)
    return pl.pallas_call(
        kern,
        grid=(B, Hk),
        in_specs=[pl.BlockSpec((1, 1, n_rows, width), lambda b, h: (b, h, 0, 0)),
                  _const_spec(pe8.shape), _const_spec(w1a.shape), _const_spec(w1b.shape),
                  _const_spec(w1.shape), _const_spec(w2p.shape), _const_spec(tail.shape)],
        out_specs=pl.BlockSpec((1, 1, CMP_FRAME + n_rows, LANES), lambda b, h: (b, h, 0, 0)),
        out_shape=jax.ShapeDtypeStruct((B, Hk, CMP_FRAME + n_rows, LANES), F32),
        compiler_params=_params(("parallel", "parallel")),
        name="compress",
    )(ch, pe8, w1a, w1b, w1, w2p, tail)


def _softmax_update(carry, s, v):
    m, acc = carry
    m_new = jnp.maximum(m, jnp.max(s, axis=-1, keepdims=True))
    p = jnp.exp2(s - m_new).astype(BF16)
    acc = acc * jnp.exp2(m - m_new) + _dot(p, v)
    return m_new, acc


def _softmax_merge(a, b):
    m = jnp.maximum(a[0], b[0])
    return m, a[1] * jnp.exp2(a[0] - m) + b[1] * jnp.exp2(b[0] - m)


def _add_on_lanes(s, lo, tile):
    hi = lo + tile.shape[1]
    parts = [s[:, :lo]] if lo else []
    parts.append(s[:, lo:hi] + tile)
    if hi < s.shape[1]:
        parts.append(s[:, hi:])
    return jnp.concatenate(parts, axis=1)


def _inv_row_sum(acc):
    return 1.0 / jnp.maximum(acc[:, COL_ONE:COL_ONE + 1], 1e-30)


def _nsa_kernel(q_ref, gate_ref, kc_ref, vc_ref, ks_ref, vs_ref, kw_ref, vw_ref,
                eye_ref, ovt_ref, wt_ref, nt_ref, ct_ref, qx_ref, pick_ref, place_ref, out_ref):
    si = pl.program_id(2)
    NB, G, QB = STEP_BLOCKS, NSA_GROUP, Q_BLOCK
    rows = NB * G * QB
    groups = [(b, g) for b in range(NB) for g in range(G)]
    row_slice = lambda b, g: slice((b * G + g) * QB, (b * G + g + 1) * QB)
    init = (jnp.full((rows, 1), M_FLOOR, F32), jnp.zeros((rows, LANES), F32))

    qblk = q_ref[0]
    q_pad = jnp.concatenate([_dot(qblk[b * QB:(b + 1) * QB], pick_ref[g]) for b, g in groups], axis=0)
    q_pad = (q_pad + qx_ref[0]).astype(BF16)
    q0 = si * STEP_Q

    c0 = pl.multiple_of(SUBLANES * NB * (si + 1), SUBLANES)
    kc = kc_ref[0, 0, pl.ds(c0, CMP_FRAME), :].astype(BF16)
    vc = vc_ref[0, 0, pl.ds(c0, CMP_FRAME), :].astype(BF16)
    s_c = _add_on_lanes(_dot_nt(q_pad, kc), CMP_FRAME - LANES, ct_ref[0])
    m_c = jnp.maximum(jnp.max(s_c, axis=-1, keepdims=True), M_FLOOR)
    e_c = jnp.exp2(s_c - m_c)
    p_c = e_c * (1.0 / jnp.maximum(jnp.sum(e_c, axis=-1, keepdims=True), 1e-30))
    o_c = _dot(p_c.astype(BF16), vc)

    p_sum = jnp.concatenate([sum(p_c[row_slice(b, g)] for g in range(G)) for b in range(NB)], axis=0)
    p_hi = p_sum.astype(BF16)
    p_lo = (p_sum - p_hi.astype(F32)).astype(BF16)
    imp_t = _dot_nt(ovt_ref[...], p_hi) + _dot_nt(ovt_ref[...], p_lo)
    jr = lax.broadcasted_iota(jnp.int32, (SEL_FRAME, STEP_Q), 0)
    ql = lax.broadcasted_iota(jnp.int32, (SEL_FRAME, STEP_Q), 1)
    j_abs = jr + (2 * NB * (si + 1) - SEL_FRAME)
    cur = (SEL_FRAME - 2 * NB) + ql // SEL_BLOCK
    exists = j_abs >= 0
    forced = exists & ((j_abs == 0) | (jr == cur) | (jr == cur - 1))
    valid = exists & (jr <= cur)
    free = valid & jnp.logical_not(forced)
    big = 3.0e38
    score = jnp.where(free, imp_t, -big)
    chosen = jnp.zeros((SEL_FRAME, STEP_Q), jnp.bool_)
    for _ in range(SEL_TOPK - 3):
        top = jnp.max(score, axis=0, keepdims=True)
        first = jnp.min(jnp.where(score == top, jr, SEL_FRAME), axis=0, keepdims=True)
        hit = jr == first
        chosen = chosen | hit
        score = jnp.where(hit, -big, score)
    neg_t = jnp.where(forced | (chosen & free), 0.0, NEG)
    neg = neg_t.T.astype(BF16)
    neg_rows = jnp.concatenate([neg[b * QB:(b + 1) * QB] for b, _ in groups], axis=0)
    q_sel = jnp.concatenate([neg_rows, q_pad], axis=1)

    r_win = q0 - WIN + KEY_PAD
    carry_w = init
    for lo in range(0, WIN + STEP_Q, FAR_CHUNK):
        n = min(FAR_CHUNK, WIN + STEP_Q - lo)
        r = pl.multiple_of(r_win + lo, LANES)
        s_w = _dot_nt(q_pad, kw_ref[0, 0, pl.ds(r, n), :]) + wt_ref[0, :, lo:lo + n]
        carry_w = _softmax_update(carry_w, s_w, vw_ref[0, 0, pl.ds(r, n), :])
    acc_w = carry_w[1]

    near_blocks = 2 * STEP_Q // SEL_BLOCK
    far_end = q0 - STEP_Q + KEY_PAD
    n_far = (jnp.maximum(q0 - STEP_Q, 0) + FAR_CHUNK - 1) // FAR_CHUNK

    def far_chunk(chain, i, live):
        r0 = pl.multiple_of(jnp.where(live, far_end - FAR_CHUNK * (i + 1), 0), LANES)
        e0 = jnp.where(live, SEL_BLOCK * (SEL_FRAME - near_blocks) - FAR_CHUNK * (i + 1) + KEY_PAD, 0)
        e0 = pl.multiple_of(e0, LANES)
        keys = jnp.concatenate([eye_ref[pl.ds(e0, FAR_CHUNK), :], ks_ref[0, 0, pl.ds(r0, FAR_CHUNK), :]], axis=1)
        s = _dot_nt(q_sel, keys)
        return _softmax_update(chain, s, vs_ref[0, 0, pl.ds(r0, FAR_CHUNK), :])

    def far_pair(i, carry):
        return far_chunk(far_chunk(carry, 2 * i, True), 2 * i + 1, 2 * i + 1 < n_far)

    r_near = pl.multiple_of(far_end, LANES)
    e_near = SEL_BLOCK * (SEL_FRAME - near_blocks) + KEY_PAD
    keys = jnp.concatenate([eye_ref[e_near:e_near + 2 * STEP_Q, :], ks_ref[0, 0, pl.ds(r_near, 2 * STEP_Q), :]],
                           axis=1)
    s_n = _dot_nt(q_sel, keys) + nt_ref[0]
    near = _softmax_update(init, s_n, vs_ref[0, 0, pl.ds(r_near, 2 * STEP_Q), :])
    _, acc_s = lax.fori_loop(0, (n_far + 1) // 2, far_pair, near)

    gates = gate_ref[0]
    def gate_col(br):
        return jnp.concatenate([gates[b * QB:(b + 1) * QB, 3 * g + br:3 * g + br + 1] for b, g in groups], axis=0)
    o = (gate_col(0) * o_c + (gate_col(1) * _inv_row_sum(acc_s)) * acc_s
         + (gate_col(2) * _inv_row_sum(acc_w)) * acc_w).astype(BF16)
    out = [sum(_dot(o[row_slice(b, g)], place_ref[g]) for g in range(G)) for b in range(NB)]
    out_ref[0] = jnp.concatenate(out, axis=0).astype(BF16)


def _nsa(q, gates, kc, vc, ks, vs, kw, vw, eye, ovt, wt, nt, ct, qx, pick, place):
    B, S, _ = q.shape
    Hk = NSA_KV_HEADS
    per_head = lambda a: pl.BlockSpec((1, 1) + a.shape[2:], lambda b, h, i: (b, h, 0, 0))
    tab = lambda a: pl.BlockSpec((1,) + a.shape[1:], lambda b, h, i: (h, 0, 0))
    qspec = pl.BlockSpec((1, STEP_Q, NSA_GROUP * HEAD_DIM), lambda b, h, i: (b, i, h))
    return pl.pallas_call(
        _nsa_kernel,
        grid=(B, Hk, S // STEP_Q),
        in_specs=[qspec, pl.BlockSpec((1, STEP_Q, LANES), lambda b, h, i: (b, i, h)),
                  per_head(kc), per_head(vc), per_head(ks), per_head(vs), per_head(kw), per_head(vw),
                  _const_spec(eye.shape), _const_spec(ovt.shape), tab(wt), tab(nt), tab(ct), tab(qx),
                  _const_spec(pick.shape), _const_spec(place.shape)],
        out_specs=qspec,
        out_shape=jax.ShapeDtypeStruct(q.shape, BF16),
        compiler_params=_params(("parallel", "parallel", "arbitrary")),
        name="nsa",
    )(q, gates, kc, vc, ks, vs, kw, vw, eye, ovt, wt, nt, ct, qx, pick, place)


CONV_HALO = 32


def _conv_kernel(cur_ref, prev_ref, w_ref, b_ref, g_ref, beta_ref, out_ref):
    i = pl.program_id(1)
    cur = cur_ref[0]
    rows = cur.shape[0]
    halo = prev_ref[0, rows - CONV_HALO:rows, :]
    halo = jnp.where(i > 0, halo, 0.0)
    ext = jnp.concatenate([halo, cur], axis=0)
    acc = b_ref[...]
    for s in range(SUBLANES):
        part = 0.0
        for a in range((CONV_K - 1 - s) // SUBLANES + 1):
            lo = CONV_HALO - SUBLANES * (a + 1)
            k = CONV_K - 1 - SUBLANES * a - s
            part = part + ext[lo:lo + rows + SUBLANES, :] * w_ref[k:k + 1, :]
        acc = acc + part[SUBLANES - s:SUBLANES - s + rows, :]
    y = _ln(acc, g_ref[...], beta_ref[...])
    out_ref[0] = (y * jax.nn.sigmoid(y)).astype(BF16)


def _conv(hglu, w, b, g, beta):
    B, S, C = hglu.shape
    rows = ROW_TILE
    return pl.pallas_call(
        _conv_kernel,
        grid=(B, S // rows),
        in_specs=[pl.BlockSpec((1, rows, C), lambda b, i: (b, i, 0)),
                  pl.BlockSpec((1, rows, C), lambda b, i: (b, jnp.maximum(i - 1, 0), 0)),
                  _const_spec(w.shape), _const_spec(b.shape), _const_spec(g.shape), _const_spec(beta.shape)],
        out_specs=pl.BlockSpec((1, rows, C), lambda b, i: (b, i, 0)),
        out_shape=jax.ShapeDtypeStruct((B, S, C), BF16),
        compiler_params=_params(("parallel", "arbitrary")),
        name="conv",
    )(hglu, hglu, w, b, g, beta)


def _out_proj_kernel(nsa_ref, conv_ref, x_ref, gin_ref, bin_ref, wa_ref, wb_ref, g_ref, b_ref, out_ref):
    h = _ln(x_ref[...], gin_ref[...], bin_ref[...])
    mix = _dot(nsa_ref[...], wa_ref[...]) + _dot(conv_ref[...], wb_ref[...])
    out_ref[...] = _ln(ALPHA * h + mix, g_ref[...], b_ref[...])


def _out_proj(o_nsa, o_conv, x2d, gin, bin_, wa, wb, g, b):
    T, D = x2d.shape
    tm = ROW_TILE
    row = lambda n: pl.BlockSpec((tm, n), lambda i: (i, 0))
    vec = _const_spec((1, D))
    return pl.pallas_call(
        _out_proj_kernel,
        grid=(T // tm,),
        in_specs=[row(o_nsa.shape[1]), row(o_conv.shape[1]), row(D), vec, vec,
                  _const_spec(wa.shape), _const_spec(wb.shape), vec, vec],
        out_specs=row(D),
        out_shape=jax.ShapeDtypeStruct((T, D), F32),
        compiler_params=_params(("parallel",)),
        name="out_proj",
    )(o_nsa, o_conv, x2d, gin, bin_, wa, wb, g, b)


def _mem_kv_kernel(m_ref, g_ref, b_ref, w_ref, out_ref):
    out_ref[...] = _dot(_ln(m_ref[...], g_ref[...], b_ref[...]).astype(BF16), w_ref[...]).astype(BF16)


def _mem_kv(mem2d, g, b, wkv):
    R, D = mem2d.shape
    N = wkv.shape[1]
    tn = 1024
    return pl.pallas_call(
        _mem_kv_kernel,
        grid=(R // ROW_TILE, N // tn),
        in_specs=[pl.BlockSpec((ROW_TILE, D), lambda i, j: (i, 0)), _const_spec((1, D)), _const_spec((1, D)),
                  pl.BlockSpec((D, tn), lambda i, j: (0, j))],
        out_specs=pl.BlockSpec((ROW_TILE, tn), lambda i, j: (i, j)),
        out_shape=jax.ShapeDtypeStruct((R, N), BF16),
        compiler_params=_params(("parallel", "parallel")),
        name="mem_kv",
    )(mem2d, g, b, wkv)


def _xattn_kernel(x_ref, kv_ref, wq_ref, wo_ref, g_ref, b_ref, rw_ref, rw_hi_ref, rb_ref,
                  x2_ref, x2p_ref, idx_ref, wt_ref):
    D = x_ref.shape[1]
    dh = D // X_HEADS
    x1 = x_ref[...]
    q = _dot(x1.astype(BF16), wq_ref[...]).astype(BF16)
    heads = []
    for h in range(X_HEADS):
        k = kv_ref[0, :, h * dh:(h + 1) * dh]
        v = kv_ref[0, :, D + h * dh:D + (h + 1) * dh]
        s = _dot_nt(q[:, h * dh:(h + 1) * dh], k) * (dh ** -0.5)
        e = jnp.exp(s - jnp.max(s, axis=-1, keepdims=True))
        p = e / jnp.sum(e, axis=-1, keepdims=True)
        heads.append(_dot(p.astype(BF16), v).astype(BF16))
    xa = _dot(jnp.concatenate(heads, axis=1), wo_ref[...])
    x2 = _ln(ALPHA * x1 + xa, g_ref[...], b_ref[...])
    x2_ref[...] = x2
    _rows_to_tiles(x2p_ref, _pack_halves(x2))

    x_hi = x2.astype(BF16)
    x_lo = (x2 - x_hi.astype(F32)).astype(BF16)
    both = _dot(x_hi, rw_ref[...])
    logit = both[:, 0:LANES] + both[:, LANES:2 * LANES] + _dot(x_lo, rw_hi_ref[...]) + rb_ref[...]
    lane = lax.broadcasted_iota(jnp.int32, logit.shape, 1)
    big = 3.0e38
    lg = jnp.where(lane < MOE_GROUPS, logit, -big)
    lg_max = jnp.max(lg, axis=-1, keepdims=True)
    grp = jnp.min(jnp.where(lg == lg_max, lane, LANES), axis=-1, keepdims=True)
    w_grp = 1.0 / jnp.sum(jnp.exp(lg - lg_max), axis=-1, keepdims=True)
    e_lane = lane - MOE_GROUPS
    in_grp = (e_lane >= grp * MOE_EPG) & (e_lane < (grp + 1) * MOE_EPG)
    le = jnp.where(in_grp, logit, -big)
    v1 = jnp.max(le, axis=-1, keepdims=True)
    i1 = jnp.min(jnp.where(le == v1, lane, LANES), axis=-1, keepdims=True)
    le2 = jnp.where(lane == i1, -big, le)
    v2 = jnp.max(le2, axis=-1, keepdims=True)
    i2 = jnp.min(jnp.where(le2 == v2, lane, LANES), axis=-1, keepdims=True)
    r = jnp.exp(v2 - v1)
    w1 = w_grp / (1.0 + r)
    w2 = w_grp * r / (1.0 + r)
    idx_ref[...] = jnp.where(lane == 0, i1 - MOE_GROUPS, jnp.where(lane == 1, i2 - MOE_GROUPS, 0))
    wt_ref[...] = jnp.where(lane == 0, w1, jnp.where(lane == 1, w2, 0.0))


def _xattn(x1, mem_kv, S, wq, wo, g, b, rw, rw_hi, rb):
    T, D = x1.shape
    tm = ROW_TILE
    row = lambda n: pl.BlockSpec((tm, n), lambda i: (i, 0))
    vec = _const_spec((1, D))
    per_step = S // tm
    return pl.pallas_call(
        _xattn_kernel,
        grid=(T // tm,),
        in_specs=[row(D), pl.BlockSpec((1,) + mem_kv.shape[1:], lambda i: (i // per_step, 0, 0)),
                  _const_spec(wq.shape), _const_spec(wo.shape), vec, vec,
                  _const_spec(rw.shape), _const_spec(rw_hi.shape), _const_spec(rb.shape)],
        out_specs=[row(D), pl.BlockSpec((tm * SUBLANES, LANES), lambda i: (i, 0)), row(LANES), row(LANES)],
        out_shape=[jax.ShapeDtypeStruct((T, D), F32), jax.ShapeDtypeStruct((T * SUBLANES, LANES), jnp.int32),
                   jax.ShapeDtypeStruct((T, LANES), jnp.int32), jax.ShapeDtypeStruct((T, LANES), F32)],
        compiler_params=_params(("parallel",)),
        name="xattn",
    )(x1, mem_kv, wq, wo, g, b, rw, rw_hi, rb)


def _tile(ref, t):
    return ref.at[pl.ds(pl.multiple_of(t * SUBLANES, SUBLANES), SUBLANES), :]


def _gather_copy(x_hbm, a, xbuf, sem, r):
    tok = jnp.maximum(a, 0) >> 1
    return pltpu.make_async_copy(_tile(x_hbm, tok), _tile(xbuf, r), sem)


def _scatter_copy(ybuf, a, y_hbm, sem, r, n_real, slot):
    dst = jnp.where(a >= 0, a, n_real + slot * MOE_ROWS + r)
    return pltpu.make_async_copy(_tile(ybuf, r), _tile(y_hbm, dst), sem)


def _moe_ffn_kernel(blk_e_ref, prev_ref, cur_ref, nxt_ref, x_hbm, wg_ref, wu_ref, wd_ref, y_hbm,
                    xbuf, ybuf, gsem, ssem):
    i = pl.program_id(0)
    last = pl.num_programs(0) - 1
    slot = i % 2
    other = 1 - slot
    R = MOE_ROWS
    n_real = 2 * (x_hbm.shape[0] // SUBLANES)
    rows = range(R)
    gather = lambda ref, s, r: _gather_copy(x_hbm, ref[0, 0, r], xbuf.at[s], gsem.at[s], r)
    scatter = lambda a, s, r: _scatter_copy(ybuf.at[s], a, y_hbm, ssem.at[s], r, n_real, s)
    gather_wait = lambda s: pltpu.make_async_copy(xbuf.at[s], xbuf.at[s], gsem.at[s]).wait()
    scatter_wait = lambda s: pltpu.make_async_copy(ybuf.at[s], ybuf.at[s], ssem.at[s]).wait()

    @pl.when(i == 0)
    def _():
        ybuf[...] = jnp.zeros_like(ybuf)
        for r in rows:
            gather(cur_ref, 0, r).start()
            scatter(-1, 0, r).start()

    gather_wait(slot)
    scatter_wait(slot)
    for r in rows:
        gather(nxt_ref, other, r).start()
        scatter(jnp.where(i > 0, prev_ref[0, 0, r], -1), other, r).start()

    xb = _unpack_halves(_tiles_to_rows(xbuf.at[slot], R))
    hid = jax.nn.silu(_dot(xb, wg_ref[0])) * _dot(xb, wu_ref[0])
    _rows_to_tiles(ybuf.at[slot], _pack_halves(_dot(hid.astype(BF16), wd_ref[0])))

    @pl.when(i == last)
    def _():
        for r in rows:
            scatter(cur_ref[0, 0, r], slot, r).start()
        gather_wait(other)
        scatter_wait(other)
        scatter_wait(slot)


def _moe_ffn(a_buf, blk_e, x2p, w_gate, w_up, w_down):
    T = x2p.shape[0] // SUBLANES
    D = 2 * SUBLANES * LANES
    R = MOE_ROWS
    n_blk = a_buf.shape[0] // R
    Hd = w_gate.shape[2]
    a3 = a_buf.reshape(n_blk, 1, R)
    idx = lambda f: pl.BlockSpec((1, 1, R), lambda i, be: (f(i), 0, 0), memory_space=pltpu.SMEM)
    grid_spec = pltpu.PrefetchScalarGridSpec(
        num_scalar_prefetch=1,
        grid=(n_blk,),
        in_specs=[idx(lambda i: jnp.maximum(i - 1, 0)), idx(lambda i: i), idx(lambda i: jnp.minimum(i + 1, n_blk - 1)),
                  pl.BlockSpec(memory_space=pl.ANY),
                  pl.BlockSpec((1, D, Hd), lambda i, be: (be[i], 0, 0)),
                  pl.BlockSpec((1, D, Hd), lambda i, be: (be[i], 0, 0)),
                  pl.BlockSpec((1, Hd, D), lambda i, be: (be[i], 0, 0))],
        out_specs=pl.BlockSpec(memory_space=pl.ANY),
        scratch_shapes=[pltpu.VMEM((2, R * SUBLANES, LANES), jnp.int32), pltpu.VMEM((2, R * SUBLANES, LANES), jnp.int32),
                        pltpu.SemaphoreType.DMA((2,)), pltpu.SemaphoreType.DMA((2,))],
    )
    return pl.pallas_call(
        _moe_ffn_kernel,
        grid_spec=grid_spec,
        out_shape=jax.ShapeDtypeStruct(((2 * T + 2 * R) * SUBLANES, LANES), jnp.int32),
        compiler_params=_params(("arbitrary",)),
        name="moe_ffn",
    )(blk_e, a3, a3, a3, x2p, w_gate, w_up, w_down)


def _combine_kernel(y_ref, wt_ref, x_ref, g_ref, b_ref, out_ref):
    rows = x_ref.shape[0]
    wt = wt_ref[...]
    y_slot = lambda k: _unpack_halves(_tiles_to_rows(y_ref, rows, k * SUBLANES, 2 * SUBLANES)).astype(F32)
    moe = wt[:, 0:1] * y_slot(0) + wt[:, 1:2] * y_slot(1)
    out_ref[...] = _ln(ALPHA * x_ref[...] + moe, g_ref[...], b_ref[...])


def _combine(y, wts, x2, g, b):
    T, D = x2.shape
    tm = ROW_TILE
    row = lambda n: pl.BlockSpec((tm, n), lambda i: (i, 0))
    return pl.pallas_call(
        _combine_kernel,
        grid=(T // tm,),
        in_specs=[pl.BlockSpec((2 * tm * SUBLANES, LANES), lambda i: (i, 0)), row(LANES), row(D),
                  _const_spec((1, D)), _const_spec((1, D))],
        out_specs=row(D),
        out_shape=jax.ShapeDtypeStruct((T, D), F32),
        compiler_params=_params(("parallel",)),
        name="moe_combine",
    )(y, wts, x2, g, b)


def _t5_bucket(dist):
    n = np.maximum(dist, 0)
    max_exact = REL_BUCKETS // 2
    nf = np.maximum(n, 1).astype(np.float64)
    large = max_exact + (np.log(nf / max_exact) / math.log(REL_MAX_DIST / max_exact)
                         * (REL_BUCKETS - max_exact)).astype(np.int64)
    large = np.minimum(large, REL_BUCKETS - 1)
    return np.where(n < max_exact, n, large)


def _bias_tiles(rel_table):
    Hk, G, QB, NB = NSA_KV_HEADS, NSA_GROUP, Q_BLOCK, STEP_BLOCKS
    tab = rel_table.reshape(REL_BUCKETS, Hk, G).transpose(1, 2, 0) * LOG2E
    far = tab[:, :, REL_BUCKETS - 1]
    far_hi = far.astype(BF16).astype(F32)
    far_lo = far - far_hi
    rel = tab - far[:, :, None]

    def sub_tile(dist, valid):
        bucket = _t5_bucket(dist)
        if not (valid & (bucket != REL_BUCKETS - 1)).any():
            return jnp.broadcast_to(jnp.asarray(np.where(valid, 0.0, NEG), F32), (Hk, G) + dist.shape)
        onehot = jnp.asarray(bucket[None] == np.arange(REL_BUCKETS)[:, None, None], F32)
        bias = jnp.einsum('hgk,kqn->hgqn', rel, onehot, precision=lax.Precision.HIGHEST)
        return jnp.where(jnp.asarray(valid), bias, NEG)

    def tile(dist_fn, valid_fn, n_keys):
        qr = np.arange(QB)[:, None]
        kk = np.arange(LANES)[None, :]
        blocks = []
        for b in range(NB):
            subs = []
            for c in range(n_keys // LANES):
                dist = dist_fn(b, qr, kk + c * LANES)
                subs.append(sub_tile(dist, valid_fn(dist)))
            blocks.append(jnp.concatenate(subs, axis=-1))
        return jnp.stack(blocks, axis=1).reshape(Hk, NB * G * QB, n_keys).astype(F32)

    wt = tile(lambda b, qr, kr: WIN + QB * b + qr - kr, lambda d: (d >= 0) & (d < WIN), WIN + STEP_Q)
    nt = tile(lambda b, qr, kr: STEP_Q + QB * b + qr - kr, lambda d: d >= 0, 2 * STEP_Q)
    frame_lo = CMP_FRAME - LANES
    ct = tile(lambda b, qr, r: (QB * (b - NB) + qr + CMP_STRIDE * (CMP_FRAME - frame_lo - r) - (CMP_BLOCK - 1)),
              lambda d: d >= 0, LANES)
    lane = np.arange(LANES)[None, None, None, :]
    qx = (jnp.where(lane == COL_ONE, far_hi[:, :, None, None], 0.0)
          + jnp.where(lane == COL_ONE2, far_lo[:, :, None, None], 0.0)
          + jnp.where(lane == COL_PAD, NEG, 0.0))
    qx = jnp.broadcast_to(qx[:, None], (Hk, NB, G, QB, LANES)).reshape(Hk, NB * G * QB, LANES)
    return wt, nt, ct, qx.astype(F32)


def _static_tables():
    G = NSA_GROUP
    rows = np.arange(SEL_FRAME * SEL_BLOCK)
    eye = (rows[:, None] // SEL_BLOCK == np.arange(SEL_FRAME)[None, :]).astype(np.float32)
    eye = np.concatenate([np.zeros((KEY_PAD, SEL_FRAME), np.float32), eye], axis=0)
    cs = np.arange(CMP_FRAME) * CMP_STRIDE
    ss = np.arange(SEL_FRAME) * SEL_BLOCK
    ov = np.clip(np.minimum(cs[:, None] + CMP_BLOCK, ss[None, :] + SEL_BLOCK)
                 - np.maximum(cs[:, None], ss[None, :]), 0, None) / CMP_BLOCK
    pick = np.zeros((G, G * HEAD_DIM, LANES), np.float32)
    place = np.zeros((G, LANES, G * HEAD_DIM), np.float32)
    for g in range(G):
        for d in range(HEAD_DIM):
            pick[g, g * HEAD_DIM + d, d] = 1.0
            place[g, d, g * HEAD_DIM + d] = 1.0
    return (jnp.asarray(eye, BF16), jnp.asarray(ov.T, BF16), jnp.asarray(pick, BF16), jnp.asarray(place, BF16))


def _key_layout(t, B, S, ones_cols, pad_col):
    Hk = NSA_KV_HEADS
    t = t.reshape(B, S, Hk, HEAD_DIM).transpose(0, 2, 1, 3)
    lane = np.arange(LANES - HEAD_DIM) + HEAD_DIM
    extra = np.isin(lane, ones_cols).astype(np.float32)
    body = jnp.concatenate([t, jnp.broadcast_to(jnp.asarray(extra, t.dtype), (B, Hk, S, LANES - HEAD_DIM))], axis=-1)
    pad_row = np.zeros((LANES,), np.float32)
    if pad_col is not None:
        pad_row[pad_col] = 1.0
    pad = jnp.broadcast_to(jnp.asarray(pad_row, t.dtype), (B, Hk, KEY_PAD, LANES))
    return jnp.concatenate([pad, body], axis=2)


def _dispatch(idx, T):
    R, E = MOE_ROWS, MOE_EXPERTS
    A = 2 * T
    id_bits = max(A - 1, 1).bit_length()
    e_flat = idx[:, 0:2].reshape(A)
    counts = jnp.sum((e_flat[:, None] == jnp.arange(E, dtype=jnp.int32)[None, :]).astype(jnp.int32), axis=0)
    n_pad = (-counts) % R
    real = (e_flat << (id_bits + 1)) | jnp.arange(A, dtype=jnp.int32)
    d_e = jnp.arange(E, dtype=jnp.int32)[:, None]
    d_r = jnp.arange(R, dtype=jnp.int32)[None, :]
    pad = jnp.where(d_r < n_pad[:, None], (d_e << (id_bits + 1)) | (1 << id_bits) | d_r,
                    (E << (id_bits + 1)) | (d_e * R + d_r))
    keys = jnp.sort(jnp.concatenate([real, pad.reshape(E * R)]))
    expert = keys >> (id_bits + 1)
    is_pad = ((keys >> id_bits) & 1) == 1
    a_buf = jnp.where(is_pad | (expert >= E), -1, keys & ((1 << id_bits) - 1))
    blk_e = jnp.minimum(expert.reshape(-1, R)[:, 0], E - 1)
    return a_buf, blk_e


def _forward(x, mem, ln_in_g, ln_in_b, w_in, cmp_pe_k, cmp_w1_k, cmp_w2_k, cmp_pe_v, cmp_w1_v, cmp_w2_v, rel_table, conv_dw_w, conv_dw_b, conv_ln_g, conv_ln_b, w_out, ln1_g, ln1_b, mem_ln_g, mem_ln_b, xa_wq, xa_wkv, xa_wo, ln2_g, ln2_b, router_group_w, router_group_b, router_expert_w, router_expert_b, moe_w_gate, moe_w_up, moe_w_down, ln3_g, ln3_b):
    B, S, D = x.shape
    T = B * S
    Hk, G, dh = NSA_KV_HEADS, NSA_GROUP, HEAD_DIM
    assert S % ROW_TILE == 0 and S % STEP_Q == 0 and SEL_TOPK * SEL_BLOCK <= S <= SEL_FRAME * SEL_BLOCK
    vec = lambda v: v.reshape(1, -1).astype(F32)
    x2d = x.reshape(T, D)

    w = w_in[0]
    nq, nkv = NSA_HEADS * dh, Hk * dh
    c_g = nq + 6 * nkv
    c_u = c_g + NSA_HEADS * 3
    cw = (w.shape[1] - c_u) // 2
    wq = w[:, :nq].astype(BF16)
    wkv = w[:, nq:c_g].astype(BF16)
    wg = jnp.pad(w[:, c_g:c_u].reshape(D, Hk, G * 3), ((0, 0), (0, 0), (0, LANES - G * 3)))
    wg = wg.reshape(D, Hk * LANES).astype(BF16)
    wa = w[:, c_u:c_u + cw].astype(BF16)
    wgt = w[:, c_u + cw:].astype(BF16)
    q, kv, gates, hglu = _in_proj(x2d, vec(ln_in_g), vec(ln_in_b), wq, wkv, wg, wa, wgt)

    def compressed(col, pe, w1, w2, ones_cols, pad_col):
        t = kv[:, col * nkv:(col + 1) * nkv].reshape(B, S, Hk, dh).transpose(0, 2, 1, 3)
        ch = t.reshape(B, Hk, S // CMP_STRIDE, CMP_STRIDE * dh)
        half = CMP_STRIDE * dh
        pe8 = jnp.pad(pe.reshape(1, CMP_BLOCK * dh), ((0, 7), (0, 0))).astype(BF16)
        w2p = jnp.pad(w2, ((0, 0), (0, LANES - dh))).astype(BF16)
        lane = np.arange(LANES)
        tail = np.zeros((8, LANES), np.float32)
        tail[0] = np.isin(lane, ones_cols)
        if pad_col is not None:
            tail[1, pad_col] = 1.0
        w1b16 = w1.astype(BF16)
        return _compress(ch, pe8, w1b16[:half], w1b16[half:], w1b16, w2p, jnp.asarray(tail))

    kc = compressed(0, cmp_pe_k[0], cmp_w1_k[0], cmp_w2_k[0], (COL_ONE, COL_ONE2), COL_PAD)
    vc = compressed(1, cmp_pe_v[0], cmp_w1_v[0], cmp_w2_v[0], (), None)

    part = lambda c: kv[:, c * nkv:(c + 1) * nkv]
    ks = _key_layout(part(2), B, S, (COL_ONE, COL_ONE2), COL_PAD)
    vs = _key_layout(part(3), B, S, (COL_ONE,), None)
    kw = _key_layout(part(4), B, S, (COL_ONE, COL_ONE2), COL_PAD)
    vw = _key_layout(part(5), B, S, (COL_ONE,), None)
    eye, ovt, pick, place = _static_tables()
    wt, nt, ct, qx = _bias_tiles(rel_table.astype(F32))
    o_nsa = _nsa(q.reshape(B, S, nq), gates.reshape(B, S, Hk * LANES), kc, vc, ks, vs, kw, vw,
                 eye, ovt, wt, nt, ct, qx, pick, place).reshape(T, nq)

    o_conv = _conv(hglu.reshape(B, S, cw), conv_dw_w[0].reshape(CONV_K, cw), vec(conv_dw_b[0]),
                   vec(conv_ln_g[0]), vec(conv_ln_b[0])).reshape(T, cw)

    wo = w_out[0].astype(BF16)
    x1 = _out_proj(o_nsa, o_conv, x2d, vec(ln_in_g), vec(ln_in_b), wo[:nq], wo[nq:], vec(ln1_g[0]), vec(ln1_b[0]))

    mkv = _mem_kv(mem.reshape(-1, D), vec(mem_ln_g[0]), vec(mem_ln_b[0]), xa_wkv[0].astype(BF16))
    mkv = mkv.reshape(B, mem.shape[1], 2 * D)
    rw = jnp.concatenate([router_group_w[0], router_expert_w[0]], axis=1).astype(F32)
    rw = jnp.pad(rw, ((0, 0), (0, LANES - rw.shape[1])))
    rb = jnp.pad(jnp.concatenate([router_group_b[0], router_expert_b[0]]).astype(F32), (0, LANES - MOE_GROUPS - MOE_EXPERTS))
    rw_hi = rw.astype(BF16)
    rw_lo = (rw - rw_hi.astype(F32)).astype(BF16)
    x2, x2p, idx, wts = _xattn(x1, mkv, S, xa_wq[0].astype(BF16), xa_wo[0].astype(BF16), vec(ln2_g[0]), vec(ln2_b[0]),
                          jnp.concatenate([rw_hi, rw_lo], axis=1), rw_hi, rb.reshape(1, LANES))

    a_buf, blk_e = _dispatch(idx, T)
    y = _moe_ffn(a_buf, blk_e, x2p, moe_w_gate[0].astype(BF16), moe_w_up[0].astype(BF16),
                 moe_w_down[0].astype(BF16))
    out = _combine(y, wts, x2, vec(ln3_g[0]), vec(ln3_b[0]))
    stages = dict(q=q, kc=kc, vc=vc, o_nsa=o_nsa, o_conv=o_conv, x1=x1, x2=x2)
    return out.reshape(B, S, D), stages


def kernel(x, mem, ln_in_g, ln_in_b, w_in, cmp_pe_k, cmp_w1_k, cmp_w2_k, cmp_pe_v, cmp_w1_v, cmp_w2_v, rel_table, conv_dw_w, conv_dw_b, conv_ln_g, conv_ln_b, w_out, ln1_g, ln1_b, mem_ln_g, mem_ln_b, xa_wq, xa_wkv, xa_wo, ln2_g, ln2_b, router_group_w, router_group_b, router_expert_w, router_expert_b, moe_w_gate, moe_w_up, moe_w_down, ln3_g, ln3_b):
    out, _ = _forward(x, mem, ln_in_g, ln_in_b, w_in, cmp_pe_k, cmp_w1_k, cmp_w2_k, cmp_pe_v, cmp_w1_v, cmp_w2_v, rel_table, conv_dw_w, conv_dw_b, conv_ln_g, conv_ln_b, w_out, ln1_g, ln1_b, mem_ln_g, mem_ln_b, xa_wq, xa_wkv, xa_wo, ln2_g, ln2_b, router_group_w, router_group_b, router_expert_w, router_expert_b, moe_w_gate, moe_w_up, moe_w_down, ln3_g, ln3_b)
    return out
```

```python
import functools
import math

import jax
import jax.numpy as jnp
import numpy as np
from jax import lax
from jax.experimental import pallas as pl
from jax.experimental.pallas import tpu as pltpu

HEAD_DIM = 64
NSA_HEADS = 16
NSA_KV_HEADS = 4
NSA_GROUP = 4
CMP_BLOCK = 32
CMP_STRIDE = 16
CMP_HIDDEN = 256
SEL_BLOCK = 64
SEL_TOPK = 16
WIN = 512
Q_BLOCK = 128
CONV_K = 31
REL_BUCKETS = 32
REL_MAX_DIST = 128
X_HEADS = 4
MOE_GROUPS = 4
MOE_EPG = 8
MOE_EXPERTS = 32
LN_EPS = 1e-5
DEPTH = 1
ALPHA = (2 * DEPTH) ** 0.25

LANES = 128
SUBLANES = 8
VMEM_LIMIT = 56 * 1024 * 1024
ROW_TILE = 256
MOE_ROWS = 256

STEP_BLOCKS = 2
STEP_Q = STEP_BLOCKS * Q_BLOCK
CMP_FRAME = 512
SEL_FRAME = 128
KEY_PAD = 512
FAR_CHUNK = 512
NEG = -float(2 ** 30)
M_FLOOR = -float(2 ** 27)
LOG2E = math.log2(math.e)
COL_ONE, COL_PAD, COL_ONE2 = 64, 65, 66

F32 = jnp.float32
BF16 = jnp.bfloat16


def _dot(a, b):
    return jnp.dot(a, b, preferred_element_type=F32)


def _dot_nt(a, b):
    return lax.dot_general(a, b, (((1,), (1,)), ((), ())), preferred_element_type=F32)


def _ln(x, g, b):
    mu = jnp.mean(x, axis=-1, keepdims=True)
    xc = x - mu
    var = jnp.mean(xc * xc, axis=-1, keepdims=True)
    return xc * lax.rsqrt(var + LN_EPS) * g + b


def _pack_halves(x):
    n = x.shape[1] // 2
    bits = lambda v: lax.bitcast_convert_type(v.astype(BF16).astype(F32), jnp.uint32)
    return lax.bitcast_convert_type((bits(x[:, :n]) >> 16) | bits(x[:, n:]), jnp.int32)


def _unpack_halves(w):
    u = lax.bitcast_convert_type(w, jnp.uint32)
    lo = lax.bitcast_convert_type(u << 16, F32).astype(BF16)
    hi = lax.bitcast_convert_type(u & jnp.uint32(0xFFFF0000), F32).astype(BF16)
    return jnp.concatenate([lo, hi], axis=1)


def _rows_to_tiles(ref, packed):
    rows = packed.shape[0]
    for s in range(SUBLANES):
        ref[pl.ds(s, rows, stride=SUBLANES), :] = packed[:, s * LANES:(s + 1) * LANES]


def _tiles_to_rows(ref, rows, first=0, step=SUBLANES):
    return jnp.concatenate([ref[pl.ds(first + s, rows, stride=step), :] for s in range(SUBLANES)], axis=1)


def _params(sem):
    return pltpu.CompilerParams(dimension_semantics=sem, vmem_limit_bytes=VMEM_LIMIT)


def _const_spec(shape):
    nd = len(shape)
    return pl.BlockSpec(shape, lambda *_: (0,) * nd)


def _in_proj_kernel(x_ref, g_ref, b_ref, wq_ref, wkv_ref, wg_ref, wa_ref, wgt_ref,
                    q_ref, kv_ref, gate_ref, glu_ref):
    h = _ln(x_ref[...], g_ref[...], b_ref[...]).astype(BF16)
    q_ref[...] = (_dot(h, wq_ref[...]) * (HEAD_DIM ** -0.5 * LOG2E)).astype(BF16)
    kv = _dot(h, wkv_ref[...])
    for ch in range(kv_ref.shape[1]):
        kv_ref[0, ch] = kv[:, ch * HEAD_DIM:(ch + 1) * HEAD_DIM].astype(BF16)
    gate_ref[...] = jax.nn.sigmoid(_dot(h, wg_ref[...]))
    glu_ref[...] = _dot(h, wa_ref[...]) * jax.nn.sigmoid(_dot(h, wgt_ref[...]))


def _in_proj(x2d, S, g, b, wq, wkv, wg, wa, wgt):
    T, D = x2d.shape
    tm = ROW_TILE
    row = lambda n: pl.BlockSpec((tm, n), lambda i: (i, 0))
    per_seq = S // tm
    n_kv = wkv.shape[1] // HEAD_DIM
    kv_spec = pl.BlockSpec((1, n_kv, tm, HEAD_DIM), lambda i: (i // per_seq, 0, i % per_seq, 0))
    return pl.pallas_call(
        _in_proj_kernel,
        grid=(T // tm,),
        in_specs=[row(D), _const_spec((1, D)), _const_spec((1, D)), _const_spec(wq.shape),
                  _const_spec(wkv.shape), _const_spec(wg.shape), _const_spec(wa.shape),
                  _const_spec(wgt.shape)],
        out_specs=[row(wq.shape[1]), kv_spec, row(wg.shape[1]), row(wa.shape[1])],
        out_shape=[jax.ShapeDtypeStruct((T, wq.shape[1]), BF16),
                   jax.ShapeDtypeStruct((T // S, n_kv, S, HEAD_DIM), BF16),
                   jax.ShapeDtypeStruct((T, wg.shape[1]), F32),
                   jax.ShapeDtypeStruct((T, wa.shape[1]), F32)],
        compiler_params=_params(("parallel",)),
        name="in_proj",
    )(x2d, g, b, wq, wkv, wg, wa, wgt)


def _compress_kernel(ch_ref, pe_ref, w1a_ref, w1b_ref, w1_ref, w2_ref, tail_ref, out_ref, *, n_rows):
    ch = ch_ref[0, 0]
    first = _dot(ch, w1a_ref[...])
    second = _dot(ch, w1b_ref[...])
    second = pltpu.roll(second, n_rows - 1, 0)
    pe_term = _dot(pe_ref[...], w1_ref[...])[0:1, :]
    hid = jax.nn.gelu(first + second + pe_term).astype(BF16)
    res = _dot(hid, w2_ref[...]) + tail_ref[0:1, :]
    out_ref[0, 0, 0:CMP_FRAME, :] = jnp.broadcast_to(tail_ref[1:2, :], (CMP_FRAME, LANES))
    out_ref[0, 0, CMP_FRAME:CMP_FRAME + n_rows, :] = res


def _compress(ch, pe8, w1a, w1b, w1, w2p, tail):
    B, Hk, n_rows, width = ch.shape
    kern = functools.partial(_compress_kernel, n_rows=n_rows)
    return pl.pallas_call(
        kern,
        grid=(B, Hk),
        in_specs=[pl.BlockSpec((1, 1, n_rows, width), lambda b, h: (b, h, 0, 0)),
                  _const_spec(pe8.shape), _const_spec(w1a.shape), _const_spec(w1b.shape),
                  _const_spec(w1.shape), _const_spec(w2p.shape), _const_spec(tail.shape)],
        out_specs=pl.BlockSpec((1, 1, CMP_FRAME + n_rows, LANES), lambda b, h: (b, h, 0, 0)),
        out_shape=jax.ShapeDtypeStruct((B, Hk, CMP_FRAME + n_rows, LANES), F32),
        compiler_params=_params(("parallel", "parallel")),
        name="compress",
    )(ch, pe8, w1a, w1b, w1, w2p, tail)


def _softmax_update(carry, s, v):
    m, acc = carry
    m_new = jnp.maximum(m, jnp.max(s, axis=-1, keepdims=True))
    p = jnp.exp2(s - m_new).astype(BF16)
    acc = acc * jnp.exp2(m - m_new) + _dot(p, v)
    return m_new, acc


def _softmax_merge(a, b):
    m = jnp.maximum(a[0], b[0])
    return m, a[1] * jnp.exp2(a[0] - m) + b[1] * jnp.exp2(b[0] - m)


def _add_on_lanes(s, lo, tile):
    hi = lo + tile.shape[1]
    parts = [s[:, :lo]] if lo else []
    parts.append(s[:, lo:hi] + tile)
    if hi < s.shape[1]:
        parts.append(s[:, hi:])
    return jnp.concatenate(parts, axis=1)


def _inv_row_sum(acc):
    return 1.0 / jnp.maximum(acc[:, COL_ONE:COL_ONE + 1], 1e-30)


def _nsa_kernel(q_ref, gate_ref, kc_ref, vc_ref, ks_ref, vs_ref, kw_ref, vw_ref,
                eye_ref, ovt_ref, wt_ref, nt_ref, ct_ref, qx_ref, pick_ref, place_ref, out_ref):
    si = pl.program_id(2)
    NB, G, QB = STEP_BLOCKS, NSA_GROUP, Q_BLOCK
    rows = NB * G * QB
    groups = [(b, g) for b in range(NB) for g in range(G)]
    row_slice = lambda b, g: slice((b * G + g) * QB, (b * G + g + 1) * QB)
    init = (jnp.full((rows, 1), M_FLOOR, F32), jnp.zeros((rows, LANES), F32))

    qblk = q_ref[0]
    q_pad = jnp.concatenate([_dot(qblk[b * QB:(b + 1) * QB], pick_ref[g]) for b, g in groups], axis=0)
    q_pad = (q_pad + qx_ref[0]).astype(BF16)
    q0 = si * STEP_Q

    c0 = pl.multiple_of(SUBLANES * NB * (si + 1), SUBLANES)
    kc = kc_ref[0, 0, pl.ds(c0, CMP_FRAME), :].astype(BF16)
    vc = vc_ref[0, 0, pl.ds(c0, CMP_FRAME), :].astype(BF16)
    s_c = _add_on_lanes(_dot_nt(q_pad, kc), CMP_FRAME - LANES, ct_ref[0])
    m_c = jnp.maximum(jnp.max(s_c, axis=-1, keepdims=True), M_FLOOR)
    e_c = jnp.exp2(s_c - m_c)
    p_c = e_c * (1.0 / jnp.maximum(jnp.sum(e_c, axis=-1, keepdims=True), 1e-30))
    o_c = _dot(p_c.astype(BF16), vc)

    p_sum = jnp.concatenate([sum(p_c[row_slice(b, g)] for g in range(G)) for b in range(NB)], axis=0)
    p_hi = p_sum.astype(BF16)
    p_lo = (p_sum - p_hi.astype(F32)).astype(BF16)
    imp_t = _dot_nt(ovt_ref[...], p_hi) + _dot_nt(ovt_ref[...], p_lo)
    jr = lax.broadcasted_iota(jnp.int32, (SEL_FRAME, STEP_Q), 0)
    ql = lax.broadcasted_iota(jnp.int32, (SEL_FRAME, STEP_Q), 1)
    j_abs = jr + (2 * NB * (si + 1) - SEL_FRAME)
    cur = (SEL_FRAME - 2 * NB) + ql // SEL_BLOCK
    exists = j_abs >= 0
    forced = exists & ((j_abs == 0) | (jr == cur) | (jr == cur - 1))
    valid = exists & (jr <= cur)
    free = valid & jnp.logical_not(forced)
    big = 3.0e38
    score = jnp.where(free, imp_t, -big)
    chosen = jnp.zeros((SEL_FRAME, STEP_Q), jnp.bool_)
    for _ in range(SEL_TOPK - 3):
        top = jnp.max(score, axis=0, keepdims=True)
        first = jnp.min(jnp.where(score == top, jr, SEL_FRAME), axis=0, keepdims=True)
        hit = jr == first
        chosen = chosen | hit
        score = jnp.where(hit, -big, score)
    neg_t = jnp.where(forced | (chosen & free), 0.0, NEG)
    neg = neg_t.T.astype(BF16)
    neg_rows = jnp.concatenate([neg[b * QB:(b + 1) * QB] for b, _ in groups], axis=0)
    q_sel = jnp.concatenate([neg_rows, q_pad], axis=1)

    r_win = q0 - WIN + KEY_PAD
    carry_w = init
    for lo in range(0, WIN + STEP_Q, FAR_CHUNK):
        n = min(FAR_CHUNK, WIN + STEP_Q - lo)
        r = pl.multiple_of(r_win + lo, LANES)
        s_w = _dot_nt(q_pad, kw_ref[0, 0, pl.ds(r, n), :]) + wt_ref[0, :, lo:lo + n]
        carry_w = _softmax_update(carry_w, s_w, vw_ref[0, 0, pl.ds(r, n), :])
    acc_w = carry_w[1]

    near_blocks = 2 * STEP_Q // SEL_BLOCK
    far_end = q0 - STEP_Q + KEY_PAD
    n_far = (jnp.maximum(q0 - STEP_Q, 0) + FAR_CHUNK - 1) // FAR_CHUNK

    def far_chunk(chain, i, live):
        r0 = pl.multiple_of(jnp.where(live, far_end - FAR_CHUNK * (i + 1), 0), LANES)
        e0 = jnp.where(live, SEL_BLOCK * (SEL_FRAME - near_blocks) - FAR_CHUNK * (i + 1) + KEY_PAD, 0)
        e0 = pl.multiple_of(e0, LANES)
        keys = jnp.concatenate([eye_ref[pl.ds(e0, FAR_CHUNK), :], ks_ref[0, 0, pl.ds(r0, FAR_CHUNK), :]], axis=1)
        s = _dot_nt(q_sel, keys)
        return _softmax_update(chain, s, vs_ref[0, 0, pl.ds(r0, FAR_CHUNK), :])

    def far_pair(i, carry):
        return far_chunk(far_chunk(carry, 2 * i, True), 2 * i + 1, 2 * i + 1 < n_far)

    r_near = pl.multiple_of(far_end, LANES)
    e_near = SEL_BLOCK * (SEL_FRAME - near_blocks) + KEY_PAD
    keys = jnp.concatenate([eye_ref[e_near:e_near + 2 * STEP_Q, :], ks_ref[0, 0, pl.ds(r_near, 2 * STEP_Q), :]],
                           axis=1)
    s_n = _dot_nt(q_sel, keys) + nt_ref[0]
    near = _softmax_update(init, s_n, vs_ref[0, 0, pl.ds(r_near, 2 * STEP_Q), :])
    _, acc_s = lax.fori_loop(0, (n_far + 1) // 2, far_pair, near)

    gates = gate_ref[0]
    def gate_col(br):
        return jnp.concatenate([gates[b * QB:(b + 1) * QB, 3 * g + br:3 * g + br + 1] for b, g in groups], axis=0)
    o = (gate_col(0) * o_c + (gate_col(1) * _inv_row_sum(acc_s)) * acc_s
         + (gate_col(2) * _inv_row_sum(acc_w)) * acc_w).astype(BF16)
    out = [sum(_dot(o[row_slice(b, g)], place_ref[g]) for g in range(G)) for b in range(NB)]
    out_ref[0] = jnp.concatenate(out, axis=0).astype(BF16)


def _nsa(q, gates, kc, vc, ks, vs, kw, vw, eye, ovt, wt, nt, ct, qx, pick, place):
    B, S, _ = q.shape
    Hk = NSA_KV_HEADS
    per_head = lambda a: pl.BlockSpec((1, 1) + a.shape[2:], lambda b, h, i: (b, h, 0, 0))
    tab = lambda a: pl.BlockSpec((1,) + a.shape[1:], lambda b, h, i: (h, 0, 0))
    qspec = pl.BlockSpec((1, STEP_Q, NSA_GROUP * HEAD_DIM), lambda b, h, i: (b, i, h))
    return pl.pallas_call(
        _nsa_kernel,
        grid=(B, Hk, S // STEP_Q),
        in_specs=[qspec, pl.BlockSpec((1, STEP_Q, LANES), lambda b, h, i: (b, i, h)),
                  per_head(kc), per_head(vc), per_head(ks), per_head(vs), per_head(kw), per_head(vw),
                  _const_spec(eye.shape), _const_spec(ovt.shape), tab(wt), tab(nt), tab(ct), tab(qx),
                  _const_spec(pick.shape), _const_spec(place.shape)],
        out_specs=qspec,
        out_shape=jax.ShapeDtypeStruct(q.shape, BF16),
        compiler_params=_params(("parallel", "parallel", "arbitrary")),
        name="nsa",
    )(q, gates, kc, vc, ks, vs, kw, vw, eye, ovt, wt, nt, ct, qx, pick, place)


CONV_HALO = 32


def _conv_kernel(cur_ref, prev_ref, w_ref, b_ref, g_ref, beta_ref, out_ref):
    i = pl.program_id(1)
    cur = cur_ref[0]
    rows = cur.shape[0]
    halo = prev_ref[0, rows - CONV_HALO:rows, :]
    halo = jnp.where(i > 0, halo, 0.0)
    ext = jnp.concatenate([halo, cur], axis=0)
    acc = b_ref[...]
    for s in range(SUBLANES):
        part = 0.0
        for a in range((CONV_K - 1 - s) // SUBLANES + 1):
            lo = CONV_HALO - SUBLANES * (a + 1)
            k = CONV_K - 1 - SUBLANES * a - s
            part = part + ext[lo:lo + rows + SUBLANES, :] * w_ref[k:k + 1, :]
        acc = acc + part[SUBLANES - s:SUBLANES - s + rows, :]
    y = _ln(acc, g_ref[...], beta_ref[...])
    out_ref[0] = (y * jax.nn.sigmoid(y)).astype(BF16)


def _conv(hglu, w, b, g, beta):
    B, S, C = hglu.shape
    rows = ROW_TILE
    return pl.pallas_call(
        _conv_kernel,
        grid=(B, S // rows),
        in_specs=[pl.BlockSpec((1, rows, C), lambda b, i: (b, i, 0)),
                  pl.BlockSpec((1, rows, C), lambda b, i: (b, jnp.maximum(i - 1, 0), 0)),
                  _const_spec(w.shape), _const_spec(b.shape), _const_spec(g.shape), _const_spec(beta.shape)],
        out_specs=pl.BlockSpec((1, rows, C), lambda b, i: (b, i, 0)),
        out_shape=jax.ShapeDtypeStruct((B, S, C), BF16),
        compiler_params=_params(("parallel", "arbitrary")),
        name="conv",
    )(hglu, hglu, w, b, g, beta)


def _out_proj_kernel(nsa_ref, conv_ref, x_ref, gin_ref, bin_ref, wa_ref, wb_ref, g_ref, b_ref, out_ref):
    h = _ln(x_ref[...], gin_ref[...], bin_ref[...])
    mix = _dot(nsa_ref[...], wa_ref[...]) + _dot(conv_ref[...], wb_ref[...])
    out_ref[...] = _ln(ALPHA * h + mix, g_ref[...], b_ref[...])


def _out_proj(o_nsa, o_conv, x2d, gin, bin_, wa, wb, g, b):
    T, D = x2d.shape
    tm = ROW_TILE
    row = lambda n: pl.BlockSpec((tm, n), lambda i: (i, 0))
    vec = _const_spec((1, D))
    return pl.pallas_call(
        _out_proj_kernel,
        grid=(T // tm,),
        in_specs=[row(o_nsa.shape[1]), row(o_conv.shape[1]), row(D), vec, vec,
                  _const_spec(wa.shape), _const_spec(wb.shape), vec, vec],
        out_specs=row(D),
        out_shape=jax.ShapeDtypeStruct((T, D), F32),
        compiler_params=_params(("parallel",)),
        name="out_proj",
    )(o_nsa, o_conv, x2d, gin, bin_, wa, wb, g, b)


def _mem_kv_kernel(m_ref, g_ref, b_ref, w_ref, out_ref):
    out_ref[...] = _dot(_ln(m_ref[...], g_ref[...], b_ref[...]).astype(BF16), w_ref[...]).astype(BF16)


def _mem_kv(mem2d, g, b, wkv):
    R, D = mem2d.shape
    N = wkv.shape[1]
    tn = 1024
    return pl.pallas_call(
        _mem_kv_kernel,
        grid=(R // ROW_TILE, N // tn),
        in_specs=[pl.BlockSpec((ROW_TILE, D), lambda i, j: (i, 0)), _const_spec((1, D)), _const_spec((1, D)),
                  pl.BlockSpec((D, tn), lambda i, j: (0, j))],
        out_specs=pl.BlockSpec((ROW_TILE, tn), lambda i, j: (i, j)),
        out_shape=jax.ShapeDtypeStruct((R, N), BF16),
        compiler_params=_params(("parallel", "parallel")),
        name="mem_kv",
    )(mem2d, g, b, wkv)


def _xattn_kernel(x_ref, kv_ref, wq_ref, wo_ref, g_ref, b_ref, rw_ref, rw_hi_ref, rb_ref,
                  x2_ref, x2p_ref, idx_ref, wt_ref):
    D = x_ref.shape[1]
    dh = D // X_HEADS
    x1 = x_ref[...]
    q = _dot(x1.astype(BF16), wq_ref[...]).astype(BF16)
    heads = []
    for h in range(X_HEADS):
        k = kv_ref[0, :, h * dh:(h + 1) * dh]
        v = kv_ref[0, :, D + h * dh:D + (h + 1) * dh]
        s = _dot_nt(q[:, h * dh:(h + 1) * dh], k) * (dh ** -0.5)
        e = jnp.exp(s - jnp.max(s, axis=-1, keepdims=True))
        p = e / jnp.sum(e, axis=-1, keepdims=True)
        heads.append(_dot(p.astype(BF16), v).astype(BF16))
    xa = _dot(jnp.concatenate(heads, axis=1), wo_ref[...])
    x2 = _ln(ALPHA * x1 + xa, g_ref[...], b_ref[...])
    x2_ref[...] = x2
    _rows_to_tiles(x2p_ref, _pack_halves(x2))

    x_hi = x2.astype(BF16)
    x_lo = (x2 - x_hi.astype(F32)).astype(BF16)
    both = _dot(x_hi, rw_ref[...])
    logit = both[:, 0:LANES] + both[:, LANES:2 * LANES] + _dot(x_lo, rw_hi_ref[...]) + rb_ref[...]
    lane = lax.broadcasted_iota(jnp.int32, logit.shape, 1)
    big = 3.0e38
    lg = jnp.where(lane < MOE_GROUPS, logit, -big)
    lg_max = jnp.max(lg, axis=-1, keepdims=True)
    grp = jnp.min(jnp.where(lg == lg_max, lane, LANES), axis=-1, keepdims=True)
    w_grp = 1.0 / jnp.sum(jnp.exp(lg - lg_max), axis=-1, keepdims=True)
    e_lane = lane - MOE_GROUPS
    in_grp = (e_lane >= grp * MOE_EPG) & (e_lane < (grp + 1) * MOE_EPG)
    le = jnp.where(in_grp, logit, -big)
    v1 = jnp.max(le, axis=-1, keepdims=True)
    i1 = jnp.min(jnp.where(le == v1, lane, LANES), axis=-1, keepdims=True)
    le2 = jnp.where(lane == i1, -big, le)
    v2 = jnp.max(le2, axis=-1, keepdims=True)
    i2 = jnp.min(jnp.where(le2 == v2, lane, LANES), axis=-1, keepdims=True)
    r = jnp.exp(v2 - v1)
    w1 = w_grp / (1.0 + r)
    w2 = w_grp * r / (1.0 + r)
    idx_ref[...] = jnp.where(lane == 0, i1 - MOE_GROUPS, jnp.where(lane == 1, i2 - MOE_GROUPS, 0))
    wt_ref[...] = jnp.where(lane == 0, w1, jnp.where(lane == 1, w2, 0.0))


def _xattn(x1, mem_kv, S, wq, wo, g, b, rw, rw_hi, rb):
    T, D = x1.shape
    tm = ROW_TILE
    row = lambda n: pl.BlockSpec((tm, n), lambda i: (i, 0))
    vec = _const_spec((1, D))
    per_step = S // tm
    return pl.pallas_call(
        _xattn_kernel,
        grid=(T // tm,),
        in_specs=[row(D), pl.BlockSpec((1,) + mem_kv.shape[1:], lambda i: (i // per_step, 0, 0)),
                  _const_spec(wq.shape), _const_spec(wo.shape), vec, vec,
                  _const_spec(rw.shape), _const_spec(rw_hi.shape), _const_spec(rb.shape)],
        out_specs=[row(D), pl.BlockSpec((tm * SUBLANES, LANES), lambda i: (i, 0)), row(LANES), row(LANES)],
        out_shape=[jax.ShapeDtypeStruct((T, D), F32), jax.ShapeDtypeStruct((T * SUBLANES, LANES), jnp.int32),
                   jax.ShapeDtypeStruct((T, LANES), jnp.int32), jax.ShapeDtypeStruct((T, LANES), F32)],
        compiler_params=_params(("parallel",)),
        name="xattn",
    )(x1, mem_kv, wq, wo, g, b, rw, rw_hi, rb)


def _tile(ref, t):
    return ref.at[pl.ds(pl.multiple_of(t * SUBLANES, SUBLANES), SUBLANES), :]


def _gather_copy(x_hbm, a, xbuf, sem, r):
    tok = jnp.maximum(a, 0) >> 1
    return pltpu.make_async_copy(_tile(x_hbm, tok), _tile(xbuf, r), sem)


def _scatter_copy(ybuf, a, y_hbm, sem, r, n_real, slot):
    dst = jnp.where(a >= 0, a, n_real + slot * MOE_ROWS + r)
    return pltpu.make_async_copy(_tile(ybuf, r), _tile(y_hbm, dst), sem)


def _moe_ffn_kernel(blk_e_ref, prev_ref, cur_ref, nxt_ref, x_hbm, wg_ref, wu_ref, wd_ref, y_hbm,
                    xbuf, ybuf, wg_sc, wu_sc, wd_sc, gsem, ssem):
    i = pl.program_id(0)

    @pl.when((i == 0) | (blk_e_ref[i] != blk_e_ref[jnp.maximum(i - 1, 0)]))
    def _():
        wg_sc[...] = wg_ref[0].astype(BF16)
        wu_sc[...] = wu_ref[0].astype(BF16)
        wd_sc[...] = wd_ref[0].astype(BF16)

    last = pl.num_programs(0) - 1
    slot = i % 2
    other = 1 - slot
    R = MOE_ROWS
    n_real = 2 * (x_hbm.shape[0] // SUBLANES)
    rows = range(R)
    gather = lambda ref, s, r: _gather_copy(x_hbm, ref[0, 0, r], xbuf.at[s], gsem.at[s], r)
    scatter = lambda a, s, r: _scatter_copy(ybuf.at[s], a, y_hbm, ssem.at[s], r, n_real, s)
    gather_wait = lambda s: pltpu.make_async_copy(xbuf.at[s], xbuf.at[s], gsem.at[s]).wait()
    scatter_wait = lambda s: pltpu.make_async_copy(ybuf.at[s], ybuf.at[s], ssem.at[s]).wait()

    @pl.when(i == 0)
    def _():
        ybuf[...] = jnp.zeros_like(ybuf)
        for r in rows:
            gather(cur_ref, 0, r).start()
            scatter(-1, 0, r).start()

    gather_wait(slot)
    scatter_wait(slot)
    for r in rows:
        gather(nxt_ref, other, r).start()
        scatter(jnp.where(i > 0, prev_ref[0, 0, r], -1), other, r).start()

    xb = _unpack_halves(_tiles_to_rows(xbuf.at[slot], R))
    hid = jax.nn.silu(_dot(xb, wg_sc[...])) * _dot(xb, wu_sc[...])
    _rows_to_tiles(ybuf.at[slot], _pack_halves(_dot(hid.astype(BF16), wd_sc[...])))

    @pl.when(i == last)
    def _():
        for r in rows:
            scatter(cur_ref[0, 0, r], slot, r).start()
        gather_wait(other)
        scatter_wait(other)
        scatter_wait(slot)


def _moe_ffn(a_buf, blk_e, x2p, w_gate, w_up, w_down):
    T = x2p.shape[0] // SUBLANES
    D = 2 * SUBLANES * LANES
    R = MOE_ROWS
    n_blk = a_buf.shape[0] // R
    Hd = w_gate.shape[2]
    a3 = a_buf.reshape(n_blk, 1, R)
    idx = lambda f: pl.BlockSpec((1, 1, R), lambda i, be: (f(i), 0, 0), memory_space=pltpu.SMEM)
    grid_spec = pltpu.PrefetchScalarGridSpec(
        num_scalar_prefetch=1,
        grid=(n_blk,),
        in_specs=[idx(lambda i: jnp.maximum(i - 1, 0)), idx(lambda i: i), idx(lambda i: jnp.minimum(i + 1, n_blk - 1)),
                  pl.BlockSpec(memory_space=pl.ANY),
                  pl.BlockSpec((1, D, Hd), lambda i, be: (be[i], 0, 0)),
                  pl.BlockSpec((1, D, Hd), lambda i, be: (be[i], 0, 0)),
                  pl.BlockSpec((1, Hd, D), lambda i, be: (be[i], 0, 0))],
        out_specs=pl.BlockSpec(memory_space=pl.ANY),
        scratch_shapes=[pltpu.VMEM((2, R * SUBLANES, LANES), jnp.int32), pltpu.VMEM((2, R * SUBLANES, LANES), jnp.int32),
                        pltpu.VMEM((D, Hd), BF16), pltpu.VMEM((D, Hd), BF16), pltpu.VMEM((Hd, D), BF16),
                        pltpu.SemaphoreType.DMA((2,)), pltpu.SemaphoreType.DMA((2,))],
    )
    return pl.pallas_call(
        _moe_ffn_kernel,
        grid_spec=grid_spec,
        out_shape=jax.ShapeDtypeStruct(((2 * T + 2 * R) * SUBLANES, LANES), jnp.int32),
        compiler_params=_params(("arbitrary",)),
        name="moe_ffn",
    )(blk_e, a3, a3, a3, x2p, w_gate, w_up, w_down)


def _combine_kernel(y_ref, wt_ref, x_ref, g_ref, b_ref, out_ref):
    rows = x_ref.shape[0]
    wt = wt_ref[...]
    y_slot = lambda k: _unpack_halves(_tiles_to_rows(y_ref, rows, k * SUBLANES, 2 * SUBLANES)).astype(F32)
    moe = wt[:, 0:1] * y_slot(0) + wt[:, 1:2] * y_slot(1)
    out_ref[...] = _ln(ALPHA * x_ref[...] + moe, g_ref[...], b_ref[...])


def _combine(y, wts, x2, g, b):
    T, D = x2.shape
    tm = ROW_TILE
    row = lambda n: pl.BlockSpec((tm, n), lambda i: (i, 0))
    return pl.pallas_call(
        _combine_kernel,
        grid=(T // tm,),
        in_specs=[pl.BlockSpec((2 * tm * SUBLANES, LANES), lambda i: (i, 0)), row(LANES), row(D),
                  _const_spec((1, D)), _const_spec((1, D))],
        out_specs=row(D),
        out_shape=jax.ShapeDtypeStruct((T, D), F32),
        compiler_params=_params(("parallel",)),
        name="moe_combine",
    )(y, wts, x2, g, b)


def _t5_bucket(dist):
    n = np.maximum(dist, 0)
    max_exact = REL_BUCKETS // 2
    nf = np.maximum(n, 1).astype(np.float64)
    large = max_exact + (np.log(nf / max_exact) / math.log(REL_MAX_DIST / max_exact)
                         * (REL_BUCKETS - max_exact)).astype(np.int64)
    large = np.minimum(large, REL_BUCKETS - 1)
    return np.where(n < max_exact, n, large)


def _bias_tiles(rel_table):
    Hk, G, QB, NB = NSA_KV_HEADS, NSA_GROUP, Q_BLOCK, STEP_BLOCKS
    tab = rel_table.reshape(REL_BUCKETS, Hk, G).transpose(1, 2, 0) * LOG2E
    far = tab[:, :, REL_BUCKETS - 1]
    far_hi = far.astype(BF16).astype(F32)
    far_lo = far - far_hi
    rel = tab - far[:, :, None]

    def sub_tile(dist, valid):
        bucket = _t5_bucket(dist)
        if not (valid & (bucket != REL_BUCKETS - 1)).any():
            return jnp.broadcast_to(jnp.asarray(np.where(valid, 0.0, NEG), F32), (Hk, G) + dist.shape)
        onehot = jnp.asarray(bucket[None] == np.arange(REL_BUCKETS)[:, None, None], F32)
        bias = jnp.einsum('hgk,kqn->hgqn', rel, onehot, precision=lax.Precision.HIGHEST)
        return jnp.where(jnp.asarray(valid), bias, NEG)

    def tile(dist_fn, valid_fn, n_keys):
        qr = np.arange(QB)[:, None]
        kk = np.arange(LANES)[None, :]
        blocks = []
        for b in range(NB):
            subs = []
            for c in range(n_keys // LANES):
                dist = dist_fn(b, qr, kk + c * LANES)
                subs.append(sub_tile(dist, valid_fn(dist)))
            blocks.append(jnp.concatenate(subs, axis=-1))
        return jnp.stack(blocks, axis=1).reshape(Hk, NB * G * QB, n_keys).astype(F32)

    wt = tile(lambda b, qr, kr: WIN + QB * b + qr - kr, lambda d: (d >= 0) & (d < WIN), WIN + STEP_Q)
    nt = tile(lambda b, qr, kr: STEP_Q + QB * b + qr - kr, lambda d: d >= 0, 2 * STEP_Q)
    frame_lo = CMP_FRAME - LANES
    ct = tile(lambda b, qr, r: (QB * (b - NB) + qr + CMP_STRIDE * (CMP_FRAME - frame_lo - r) - (CMP_BLOCK - 1)),
              lambda d: d >= 0, LANES)
    lane = np.arange(LANES)[None, None, None, :]
    qx = (jnp.where(lane == COL_ONE, far_hi[:, :, None, None], 0.0)
          + jnp.where(lane == COL_ONE2, far_lo[:, :, None, None], 0.0)
          + jnp.where(lane == COL_PAD, NEG, 0.0))
    qx = jnp.broadcast_to(qx[:, None], (Hk, NB, G, QB, LANES)).reshape(Hk, NB * G * QB, LANES)
    return wt, nt, ct, qx.astype(F32)


def _static_tables():
    G = NSA_GROUP
    rows = np.arange(SEL_FRAME * SEL_BLOCK)
    eye = (rows[:, None] // SEL_BLOCK == np.arange(SEL_FRAME)[None, :]).astype(np.float32)
    eye = np.concatenate([np.zeros((KEY_PAD, SEL_FRAME), np.float32), eye], axis=0)
    cs = np.arange(CMP_FRAME) * CMP_STRIDE
    ss = np.arange(SEL_FRAME) * SEL_BLOCK
    ov = np.clip(np.minimum(cs[:, None] + CMP_BLOCK, ss[None, :] + SEL_BLOCK)
                 - np.maximum(cs[:, None], ss[None, :]), 0, None) / CMP_BLOCK
    pick = np.zeros((G, G * HEAD_DIM, LANES), np.float32)
    place = np.zeros((G, LANES, G * HEAD_DIM), np.float32)
    for g in range(G):
        for d in range(HEAD_DIM):
            pick[g, g * HEAD_DIM + d, d] = 1.0
            place[g, d, g * HEAD_DIM + d] = 1.0
    return (jnp.asarray(eye, BF16), jnp.asarray(ov.T, BF16), jnp.asarray(pick, BF16), jnp.asarray(place, BF16))


def _key_layout(t, B, S, ones_cols, pad_col):
    Hk = NSA_KV_HEADS
    lane = np.arange(LANES - HEAD_DIM) + HEAD_DIM
    extra = np.isin(lane, ones_cols).astype(np.float32)
    body = jnp.concatenate([t, jnp.broadcast_to(jnp.asarray(extra, t.dtype), (B, Hk, S, LANES - HEAD_DIM))], axis=-1)
    pad_row = np.zeros((LANES,), np.float32)
    if pad_col is not None:
        pad_row[pad_col] = 1.0
    pad = jnp.broadcast_to(jnp.asarray(pad_row, t.dtype), (B, Hk, KEY_PAD, LANES))
    return jnp.concatenate([pad, body], axis=2)


def _dispatch(idx, T):
    R, E = MOE_ROWS, MOE_EXPERTS
    A = 2 * T
    id_bits = max(A - 1, 1).bit_length()
    e_flat = idx[:, 0:2].reshape(A)
    counts = jnp.sum((e_flat[:, None] == jnp.arange(E, dtype=jnp.int32)[None, :]).astype(jnp.int32), axis=0)
    n_pad = (-counts) % R
    real = (e_flat << (id_bits + 1)) | jnp.arange(A, dtype=jnp.int32)
    d_e = jnp.arange(E, dtype=jnp.int32)[:, None]
    d_r = jnp.arange(R, dtype=jnp.int32)[None, :]
    pad = jnp.where(d_r < n_pad[:, None], (d_e << (id_bits + 1)) | (1 << id_bits) | d_r,
                    (E << (id_bits + 1)) | (d_e * R + d_r))
    keys = jnp.sort(jnp.concatenate([real, pad.reshape(E * R)]))
    expert = keys >> (id_bits + 1)
    is_pad = ((keys >> id_bits) & 1) == 1
    a_buf = jnp.where(is_pad | (expert >= E), -1, keys & ((1 << id_bits) - 1))
    blk_e = jnp.minimum(expert.reshape(-1, R)[:, 0], E - 1)
    return a_buf, blk_e


def _forward(x, mem, ln_in_g, ln_in_b, w_in, cmp_pe_k, cmp_w1_k, cmp_w2_k, cmp_pe_v, cmp_w1_v, cmp_w2_v, rel_table, conv_dw_w, conv_dw_b, conv_ln_g, conv_ln_b, w_out, ln1_g, ln1_b, mem_ln_g, mem_ln_b, xa_wq, xa_wkv, xa_wo, ln2_g, ln2_b, router_group_w, router_group_b, router_expert_w, router_expert_b, moe_w_gate, moe_w_up, moe_w_down, ln3_g, ln3_b):
    B, S, D = x.shape
    T = B * S
    Hk, G, dh = NSA_KV_HEADS, NSA_GROUP, HEAD_DIM
    assert S % ROW_TILE == 0 and S % STEP_Q == 0 and SEL_TOPK * SEL_BLOCK <= S <= SEL_FRAME * SEL_BLOCK
    vec = lambda v: v.reshape(1, -1).astype(F32)
    x2d = x.reshape(T, D)

    w = w_in[0]
    nq, nkv = NSA_HEADS * dh, Hk * dh
    c_g = nq + 6 * nkv
    c_u = c_g + NSA_HEADS * 3
    cw = (w.shape[1] - c_u) // 2
    wq = w[:, :nq].astype(BF16)
    wkv = w[:, nq:c_g].astype(BF16)
    wg = jnp.pad(w[:, c_g:c_u].reshape(D, Hk, G * 3), ((0, 0), (0, 0), (0, LANES - G * 3)))
    wg = wg.reshape(D, Hk * LANES).astype(BF16)
    wa = w[:, c_u:c_u + cw].astype(BF16)
    wgt = w[:, c_u + cw:].astype(BF16)
    q, kv, gates, hglu = _in_proj(x2d, S, vec(ln_in_g), vec(ln_in_b), wq, wkv, wg, wa, wgt)
    part = lambda c: kv[:, c * Hk:(c + 1) * Hk]

    def compressed(col, pe, w1, w2, ones_cols, pad_col):
        ch = part(col).reshape(B, Hk, S // CMP_STRIDE, CMP_STRIDE * dh)
        half = CMP_STRIDE * dh
        pe8 = jnp.pad(pe.reshape(1, CMP_BLOCK * dh), ((0, 7), (0, 0))).astype(BF16)
        w2p = jnp.pad(w2, ((0, 0), (0, LANES - dh))).astype(BF16)
        lane = np.arange(LANES)
        tail = np.zeros((8, LANES), np.float32)
        tail[0] = np.isin(lane, ones_cols)
        if pad_col is not None:
            tail[1, pad_col] = 1.0
        w1b16 = w1.astype(BF16)
        return _compress(ch, pe8, w1b16[:half], w1b16[half:], w1b16, w2p, jnp.asarray(tail))

    kc = compressed(0, cmp_pe_k[0], cmp_w1_k[0], cmp_w2_k[0], (COL_ONE, COL_ONE2), COL_PAD)
    vc = compressed(1, cmp_pe_v[0], cmp_w1_v[0], cmp_w2_v[0], (), None)

    ks = _key_layout(part(2), B, S, (COL_ONE, COL_ONE2), COL_PAD)
    vs = _key_layout(part(3), B, S, (COL_ONE,), None)
    kw = _key_layout(part(4), B, S, (COL_ONE, COL_ONE2), COL_PAD)
    vw = _key_layout(part(5), B, S, (COL_ONE,), None)
    eye, ovt, pick, place = _static_tables()
    wt, nt, ct, qx = _bias_tiles(rel_table.astype(F32))
    o_nsa = _nsa(q.reshape(B, S, nq), gates.reshape(B, S, Hk * LANES), kc, vc, ks, vs, kw, vw,
                 eye, ovt, wt, nt, ct, qx, pick, place).reshape(T, nq)

    o_conv = _conv(hglu.reshape(B, S, cw), conv_dw_w[0].reshape(CONV_K, cw), vec(conv_dw_b[0]),
                   vec(conv_ln_g[0]), vec(conv_ln_b[0])).reshape(T, cw)

    wo = w_out[0].astype(BF16)
    x1 = _out_proj(o_nsa, o_conv, x2d, vec(ln_in_g), vec(ln_in_b), wo[:nq], wo[nq:], vec(ln1_g[0]), vec(ln1_b[0]))

    mkv = _mem_kv(mem.reshape(-1, D), vec(mem_ln_g[0]), vec(mem_ln_b[0]), xa_wkv[0].astype(BF16))
    mkv = mkv.reshape(B, mem.shape[1], 2 * D)
    rw = jnp.concatenate([router_group_w[0], router_expert_w[0]], axis=1).astype(F32)
    rw = jnp.pad(rw, ((0, 0), (0, LANES - rw.shape[1])))
    rb = jnp.pad(jnp.concatenate([router_group_b[0], router_expert_b[0]]).astype(F32), (0, LANES - MOE_GROUPS - MOE_EXPERTS))
    rw_hi = rw.astype(BF16)
    rw_lo = (rw - rw_hi.astype(F32)).astype(BF16)
    x2, x2p, idx, wts = _xattn(x1, mkv, S, xa_wq[0].astype(BF16), xa_wo[0].astype(BF16), vec(ln2_g[0]), vec(ln2_b[0]),
                          jnp.concatenate([rw_hi, rw_lo], axis=1), rw_hi, rb.reshape(1, LANES))

    a_buf, blk_e = _dispatch(idx, T)
    y = _moe_ffn(a_buf, blk_e, x2p, moe_w_gate[0], moe_w_up[0], moe_w_down[0])
    out = _combine(y, wts, x2, vec(ln3_g[0]), vec(ln3_b[0]))
    stages = dict(q=q, kc=kc, vc=vc, o_nsa=o_nsa, o_conv=o_conv, x1=x1, x2=x2)
    return out.reshape(B, S, D), stages


def kernel(x, mem, ln_in_g, ln_in_b, w_in, cmp_pe_k, cmp_w1_k, cmp_w2_k, cmp_pe_v, cmp_w1_v, cmp_w2_v, rel_table, conv_dw_w, conv_dw_b, conv_ln_g, conv_ln_b, w_out, ln1_g, ln1_b, mem_ln_g, mem_ln_b, xa_wq, xa_wkv, xa_wo, ln2_g, ln2_b, router_group_w, router_group_b, router_expert_w, router_expert_b, moe_w_gate, moe_w_up, moe_w_down, ln3_g, ln3_b):
    out, _ = _forward(x, mem, ln_in_g, ln_in_b, w_in, cmp_pe_k, cmp_w1_k, cmp_w2_k, cmp_pe_v, cmp_w1_v, cmp_w2_v, rel_table, conv_dw_w, conv_dw_b, conv_ln_g, conv_ln_b, w_out, ln1_g, ln1_b, mem_ln_g, mem_ln_b, xa_wq, xa_wkv, xa_wo, ln2_g, ln2_b, router_group_w, router_group_b, router_expert_w, router_expert_b, moe_w_gate, moe_w_up, moe_w_down, ln3_g, ln3_b)
    return out
```

```python
import functools
import math

import jax
import jax.numpy as jnp
import numpy as np
from jax import lax
from jax.experimental import pallas as pl
from jax.experimental.pallas import tpu as pltpu

HEAD_DIM = 64
NSA_HEADS = 16
NSA_KV_HEADS = 4
NSA_GROUP = 4
CMP_BLOCK = 32
CMP_STRIDE = 16
CMP_HIDDEN = 256
SEL_BLOCK = 64
SEL_TOPK = 16
WIN = 512
Q_BLOCK = 128
CONV_K = 31
REL_BUCKETS = 32
REL_MAX_DIST = 128
X_HEADS = 4
MOE_GROUPS = 4
MOE_EPG = 8
MOE_EXPERTS = 32
LN_EPS = 1e-5
DEPTH = 1
ALPHA = (2 * DEPTH) ** 0.25

LANES = 128
SUBLANES = 8
VMEM_LIMIT = 56 * 1024 * 1024
ROW_TILE = 256
MOE_ROWS = 256

STEP_BLOCKS = 2
STEP_Q = STEP_BLOCKS * Q_BLOCK
CMP_FRAME = 512
SEL_FRAME = 128
KEY_PAD = 512
FAR_CHUNK = 512
NEG = -float(2 ** 30)
M_FLOOR = -float(2 ** 27)
LOG2E = math.log2(math.e)
COL_ONE, COL_PAD, COL_ONE2 = 64, 65, 66

F32 = jnp.float32
BF16 = jnp.bfloat16


def _dot(a, b):
    return jnp.dot(a, b, preferred_element_type=F32)


def _dot_nt(a, b):
    return lax.dot_general(a, b, (((1,), (1,)), ((), ())), preferred_element_type=F32)


def _ln(x, g, b):
    mu = jnp.mean(x, axis=-1, keepdims=True)
    xc = x - mu
    var = jnp.mean(xc * xc, axis=-1, keepdims=True)
    return xc * lax.rsqrt(var + LN_EPS) * g + b


def _pack_halves(x):
    n = x.shape[1] // 2
    bits = lambda v: lax.bitcast_convert_type(v.astype(BF16).astype(F32), jnp.uint32)
    return lax.bitcast_convert_type((bits(x[:, :n]) >> 16) | bits(x[:, n:]), jnp.int32)


def _unpack_halves(w):
    u = lax.bitcast_convert_type(w, jnp.uint32)
    lo = lax.bitcast_convert_type(u << 16, F32).astype(BF16)
    hi = lax.bitcast_convert_type(u & jnp.uint32(0xFFFF0000), F32).astype(BF16)
    return jnp.concatenate([lo, hi], axis=1)


def _rows_to_tiles(ref, packed):
    rows = packed.shape[0]
    for s in range(SUBLANES):
        ref[pl.ds(s, rows, stride=SUBLANES), :] = packed[:, s * LANES:(s + 1) * LANES]


def _tiles_to_rows(ref, rows, first=0, step=SUBLANES):
    return jnp.concatenate([ref[pl.ds(first + s, rows, stride=step), :] for s in range(SUBLANES)], axis=1)


def _params(sem):
    return pltpu.CompilerParams(dimension_semantics=sem, vmem_limit_bytes=VMEM_LIMIT)


def _const_spec(shape):
    nd = len(shape)
    return pl.BlockSpec(shape, lambda *_: (0,) * nd)


def _in_proj_kernel(x_ref, g_ref, b_ref, wq_ref, wkv_ref, wg_ref, wa_ref, wgt_ref, ks_in, vs_in, kw_in, vw_in,
                    q_ref, kc_ref, vc_ref, ks_ref, vs_ref, kw_ref, vw_ref, gate_ref, glu_ref):
    del ks_in, vs_in, kw_in, vw_in
    Hk, dh = NSA_KV_HEADS, HEAD_DIM
    h = _ln(x_ref[...], g_ref[...], b_ref[...]).astype(BF16)
    q_ref[...] = (_dot(h, wq_ref[...]) * (dh ** -0.5 * LOG2E)).astype(BF16)
    kv = _dot(h, wkv_ref[...])
    for c, ref in enumerate((kc_ref, vc_ref)):
        for hh in range(Hk):
            ref[0, hh] = kv[:, (c * Hk + hh) * dh:(c * Hk + hh + 1) * dh].astype(BF16)
    lane = lax.broadcasted_iota(jnp.int32, (kv.shape[0], LANES), 1)
    k_extra = jnp.where((lane == COL_ONE) | (lane == COL_ONE2), 1.0, 0.0)
    v_extra = jnp.where(lane == COL_ONE, 1.0, 0.0)
    for c, ref, extra in ((2, ks_ref, k_extra), (3, vs_ref, v_extra), (4, kw_ref, k_extra), (5, vw_ref, v_extra)):
        for pair in range(Hk // 2):
            both = kv[:, (c * Hk + 2 * pair) * dh:(c * Hk + 2 * pair + 2) * dh]
            ref[0, 2 * pair] = jnp.where(lane < dh, both, extra).astype(BF16)
            ref[0, 2 * pair + 1] = jnp.where(lane < dh, pltpu.roll(both, dh, 1), extra).astype(BF16)
    gate_ref[...] = jax.nn.sigmoid(_dot(h, wg_ref[...]))
    glu_ref[...] = _dot(h, wa_ref[...]) * jax.nn.sigmoid(_dot(h, wgt_ref[...]))


def _key_template(shape, pad_col):
    pad_row = np.zeros((LANES,), np.float32)
    if pad_col is not None:
        pad_row[pad_col] = 1.0
    front = jnp.broadcast_to(jnp.asarray(pad_row, shape.dtype), shape.shape[:2] + (KEY_PAD, LANES))
    body = jnp.zeros(shape.shape[:2] + (shape.shape[2] - KEY_PAD, LANES), shape.dtype)
    return jnp.concatenate([front, body], axis=2)


def _in_proj(x2d, S, g, b, wq, wkv, wg, wa, wgt):
    T, D = x2d.shape
    tm = ROW_TILE
    row = lambda n: pl.BlockSpec((tm, n), lambda i: (i, 0))
    per_seq = S // tm
    B, Hk = T // S, NSA_KV_HEADS
    assert KEY_PAD % tm == 0
    cmp_spec = pl.BlockSpec((1, Hk, tm, HEAD_DIM), lambda i: (i // per_seq, 0, i % per_seq, 0))
    key_spec = pl.BlockSpec((1, Hk, tm, LANES), lambda i: (i // per_seq, 0, i % per_seq + KEY_PAD // tm, 0))
    cmp_shape = jax.ShapeDtypeStruct((B, Hk, S, HEAD_DIM), BF16)
    key_shape = jax.ShapeDtypeStruct((B, Hk, KEY_PAD + S, LANES), BF16)
    return pl.pallas_call(
        _in_proj_kernel,
        grid=(T // tm,),
        in_specs=[row(D), _const_spec((1, D)), _const_spec((1, D)), _const_spec(wq.shape),
                  _const_spec(wkv.shape), _const_spec(wg.shape), _const_spec(wa.shape),
                  _const_spec(wgt.shape)] + [pl.BlockSpec(memory_space=pl.ANY)] * 4,
        input_output_aliases={8: 3, 9: 4, 10: 5, 11: 6},
        out_specs=[row(wq.shape[1]), cmp_spec, cmp_spec, key_spec, key_spec, key_spec, key_spec,
                   row(wg.shape[1]), row(wa.shape[1])],
        out_shape=[jax.ShapeDtypeStruct((T, wq.shape[1]), BF16),
                   cmp_shape, cmp_shape, key_shape, key_shape, key_shape, key_shape,
                   jax.ShapeDtypeStruct((T, wg.shape[1]), F32),
                   jax.ShapeDtypeStruct((T, wa.shape[1]), F32)],
        compiler_params=_params(("parallel",)),
        name="in_proj",
    )(x2d, g, b, wq, wkv, wg, wa, wgt, _key_template(key_shape, COL_PAD), _key_template(key_shape, None),
      _key_template(key_shape, COL_PAD), _key_template(key_shape, None))


def _compress_kernel(ch_ref, pe_ref, w1a_ref, w1b_ref, w1_ref, w2_ref, tail_ref, out_ref, *, n_rows):
    ch = ch_ref[0, 0]
    first = _dot(ch, w1a_ref[...])
    second = _dot(ch, w1b_ref[...])
    second = pltpu.roll(second, n_rows - 1, 0)
    pe_term = _dot(pe_ref[...], w1_ref[...])[0:1, :]
    hid = jax.nn.gelu(first + second + pe_term).astype(BF16)
    res = _dot(hid, w2_ref[...]) + tail_ref[0:1, :]
    out_ref[0, 0, 0:CMP_FRAME, :] = jnp.broadcast_to(tail_ref[1:2, :], (CMP_FRAME, LANES))
    out_ref[0, 0, CMP_FRAME:CMP_FRAME + n_rows, :] = res


def _compress(ch, pe8, w1a, w1b, w1, w2p, tail):
    B, Hk, n_rows, width = ch.shape
    kern = functools.partial(_compress_kernel, n_rows=n_rows)
    return pl.pallas_call(
        kern,
        grid=(B, Hk),
        in_specs=[pl.BlockSpec((1, 1, n_rows, width), lambda b, h: (b, h, 0, 0)),
                  _const_spec(pe8.shape), _const_spec(w1a.shape), _const_spec(w1b.shape),
                  _const_spec(w1.shape), _const_spec(w2p.shape), _const_spec(tail.shape)],
        out_specs=pl.BlockSpec((1, 1, CMP_FRAME + n_rows, LANES), lambda b, h: (b, h, 0, 0)),
        out_shape=jax.ShapeDtypeStruct((B, Hk, CMP_FRAME + n_rows, LANES), F32),
        compiler_params=_params(("parallel", "parallel")),
        name="compress",
    )(ch, pe8, w1a, w1b, w1, w2p, tail)


def _softmax_update(carry, s, v):
    m, acc = carry
    m_new = jnp.maximum(m, jnp.max(s, axis=-1, keepdims=True))
    p = jnp.exp2(s - m_new).astype(BF16)
    acc = acc * jnp.exp2(m - m_new) + _dot(p, v)
    return m_new, acc


def _softmax_merge(a, b):
    m = jnp.maximum(a[0], b[0])
    return m, a[1] * jnp.exp2(a[0] - m) + b[1] * jnp.exp2(b[0] - m)


def _add_on_lanes(s, lo, tile):
    hi = lo + tile.shape[1]
    parts = [s[:, :lo]] if lo else []
    parts.append(s[:, lo:hi] + tile)
    if hi < s.shape[1]:
        parts.append(s[:, hi:])
    return jnp.concatenate(parts, axis=1)


def _inv_row_sum(acc):
    return 1.0 / jnp.maximum(acc[:, COL_ONE:COL_ONE + 1], 1e-30)


def _nsa_kernel(q_ref, gate_ref, kc_ref, vc_ref, ks_ref, vs_ref, kw_ref, vw_ref,
                eye_ref, ovt_ref, wt_ref, nt_ref, ct_ref, qx_ref, pick_ref, place_ref, out_ref):
    si = pl.program_id(2)
    NB, G, QB = STEP_BLOCKS, NSA_GROUP, Q_BLOCK
    rows = NB * G * QB
    groups = [(b, g) for b in range(NB) for g in range(G)]
    row_slice = lambda b, g: slice((b * G + g) * QB, (b * G + g + 1) * QB)
    init = (jnp.full((rows, 1), M_FLOOR, F32), jnp.zeros((rows, LANES), F32))

    qblk = q_ref[0]
    q_pad = jnp.concatenate([_dot(qblk[b * QB:(b + 1) * QB], pick_ref[g]) for b, g in groups], axis=0)
    q_pad = (q_pad + qx_ref[0]).astype(BF16)
    q0 = si * STEP_Q

    c0 = pl.multiple_of(SUBLANES * NB * (si + 1), SUBLANES)
    kc = kc_ref[0, 0, pl.ds(c0, CMP_FRAME), :].astype(BF16)
    vc = vc_ref[0, 0, pl.ds(c0, CMP_FRAME), :].astype(BF16)
    s_c = _add_on_lanes(_dot_nt(q_pad, kc), CMP_FRAME - LANES, ct_ref[0])
    m_c = jnp.maximum(jnp.max(s_c, axis=-1, keepdims=True), M_FLOOR)
    e_c = jnp.exp2(s_c - m_c)
    p_c = e_c * (1.0 / jnp.maximum(jnp.sum(e_c, axis=-1, keepdims=True), 1e-30))
    o_c = _dot(p_c.astype(BF16), vc)

    p_sum = jnp.concatenate([sum(p_c[row_slice(b, g)] for g in range(G)) for b in range(NB)], axis=0)
    p_hi = p_sum.astype(BF16)
    p_lo = (p_sum - p_hi.astype(F32)).astype(BF16)
    imp_t = _dot_nt(ovt_ref[...], p_hi) + _dot_nt(ovt_ref[...], p_lo)
    jr = lax.broadcasted_iota(jnp.int32, (SEL_FRAME, STEP_Q), 0)
    ql = lax.broadcasted_iota(jnp.int32, (SEL_FRAME, STEP_Q), 1)
    j_abs = jr + (2 * NB * (si + 1) - SEL_FRAME)
    cur = (SEL_FRAME - 2 * NB) + ql // SEL_BLOCK
    exists = j_abs >= 0
    forced = exists & ((j_abs == 0) | (jr == cur) | (jr == cur - 1))
    valid = exists & (jr <= cur)
    free = valid & jnp.logical_not(forced)
    big = 3.0e38
    score = jnp.where(free, imp_t, -big)
    chosen = jnp.zeros((SEL_FRAME, STEP_Q), jnp.bool_)
    for _ in range(SEL_TOPK - 3):
        top = jnp.max(score, axis=0, keepdims=True)
        first = jnp.min(jnp.where(score == top, jr, SEL_FRAME), axis=0, keepdims=True)
        hit = jr == first
        chosen = chosen | hit
        score = jnp.where(hit, -big, score)
    neg_t = jnp.where(forced | (chosen & free), 0.0, NEG)
    neg = neg_t.T.astype(BF16)
    neg_rows = jnp.concatenate([neg[b * QB:(b + 1) * QB] for b, _ in groups], axis=0)
    q_sel = jnp.concatenate([neg_rows, q_pad], axis=1)

    r_win = q0 - WIN + KEY_PAD
    carry_w = init
    for lo in range(0, WIN + STEP_Q, FAR_CHUNK):
        n = min(FAR_CHUNK, WIN + STEP_Q - lo)
        r = pl.multiple_of(r_win + lo, LANES)
        s_w = _dot_nt(q_pad, kw_ref[0, 0, pl.ds(r, n), :]) + wt_ref[0, :, lo:lo + n]
        carry_w = _softmax_update(carry_w, s_w, vw_ref[0, 0, pl.ds(r, n), :])
    acc_w = carry_w[1]

    near_blocks = 2 * STEP_Q // SEL_BLOCK
    far_end = q0 - STEP_Q + KEY_PAD
    n_far = (jnp.maximum(q0 - STEP_Q, 0) + FAR_CHUNK - 1) // FAR_CHUNK

    def far_chunk(chain, i, live):
        r0 = pl.multiple_of(jnp.where(live, far_end - FAR_CHUNK * (i + 1), 0), LANES)
        e0 = jnp.where(live, SEL_BLOCK * (SEL_FRAME - near_blocks) - FAR_CHUNK * (i + 1) + KEY_PAD, 0)
        e0 = pl.multiple_of(e0, LANES)
        keys = jnp.concatenate([eye_ref[pl.ds(e0, FAR_CHUNK), :], ks_ref[0, 0, pl.ds(r0, FAR_CHUNK), :]], axis=1)
        s = _dot_nt(q_sel, keys)
        return _softmax_update(chain, s, vs_ref[0, 0, pl.ds(r0, FAR_CHUNK), :])

    def far_pair(i, carry):
        return far_chunk(far_chunk(carry, 2 * i, True), 2 * i + 1, 2 * i + 1 < n_far)

    r_near = pl.multiple_of(far_end, LANES)
    e_near = SEL_BLOCK * (SEL_FRAME - near_blocks) + KEY_PAD
    keys = jnp.concatenate([eye_ref[e_near:e_near + 2 * STEP_Q, :], ks_ref[0, 0, pl.ds(r_near, 2 * STEP_Q), :]],
                           axis=1)
    s_n = _dot_nt(q_sel, keys) + nt_ref[0]
    near = _softmax_update(init, s_n, vs_ref[0, 0, pl.ds(r_near, 2 * STEP_Q), :])
    _, acc_s = lax.fori_loop(0, (n_far + 1) // 2, far_pair, near)

    gates = gate_ref[0]
    def gate_col(br):
        return jnp.concatenate([gates[b * QB:(b + 1) * QB, 3 * g + br:3 * g + br + 1] for b, g in groups], axis=0)
    o = (gate_col(0) * o_c + (gate_col(1) * _inv_row_sum(acc_s)) * acc_s
         + (gate_col(2) * _inv_row_sum(acc_w)) * acc_w).astype(BF16)
    out = [sum(_dot(o[row_slice(b, g)], place_ref[g]) for g in range(G)) for b in range(NB)]
    out_ref[0] = jnp.concatenate(out, axis=0).astype(BF16)


def _nsa(q, gates, kc, vc, ks, vs, kw, vw, eye, ovt, wt, nt, ct, qx, pick, place):
    B, S, _ = q.shape
    Hk = NSA_KV_HEADS
    per_head = lambda a: pl.BlockSpec((1, 1) + a.shape[2:], lambda b, h, i: (b, h, 0, 0))
    tab = lambda a: pl.BlockSpec((1,) + a.shape[1:], lambda b, h, i: (h, 0, 0))
    qspec = pl.BlockSpec((1, STEP_Q, NSA_GROUP * HEAD_DIM), lambda b, h, i: (b, i, h))
    return pl.pallas_call(
        _nsa_kernel,
        grid=(B, Hk, S // STEP_Q),
        in_specs=[qspec, pl.BlockSpec((1, STEP_Q, LANES), lambda b, h, i: (b, i, h)),
                  per_head(kc), per_head(vc), per_head(ks), per_head(vs), per_head(kw), per_head(vw),
                  _const_spec(eye.shape), _const_spec(ovt.shape), tab(wt), tab(nt), tab(ct), tab(qx),
                  _const_spec(pick.shape), _const_spec(place.shape)],
        out_specs=qspec,
        out_shape=jax.ShapeDtypeStruct(q.shape, BF16),
        compiler_params=_params(("parallel", "parallel", "arbitrary")),
        name="nsa",
    )(q, gates, kc, vc, ks, vs, kw, vw, eye, ovt, wt, nt, ct, qx, pick, place)


CONV_HALO = 32


def _conv_kernel(cur_ref, prev_ref, w_ref, b_ref, g_ref, beta_ref, out_ref):
    i = pl.program_id(1)
    cur = cur_ref[0]
    rows = cur.shape[0]
    halo = prev_ref[0, rows - CONV_HALO:rows, :]
    halo = jnp.where(i > 0, halo, 0.0)
    ext = jnp.concatenate([halo, cur], axis=0)
    acc = b_ref[...]
    for s in range(SUBLANES):
        part = 0.0
        for a in range((CONV_K - 1 - s) // SUBLANES + 1):
            lo = CONV_HALO - SUBLANES * (a + 1)
            k = CONV_K - 1 - SUBLANES * a - s
            part = part + ext[lo:lo + rows + SUBLANES, :] * w_ref[k:k + 1, :]
        acc = acc + part[SUBLANES - s:SUBLANES - s + rows, :]
    y = _ln(acc, g_ref[...], beta_ref[...])
    out_ref[0] = (y * jax.nn.sigmoid(y)).astype(BF16)


def _conv(hglu, w, b, g, beta):
    B, S, C = hglu.shape
    rows = ROW_TILE
    return pl.pallas_call(
        _conv_kernel,
        grid=(B, S // rows),
        in_specs=[pl.BlockSpec((1, rows, C), lambda b, i: (b, i, 0)),
                  pl.BlockSpec((1, rows, C), lambda b, i: (b, jnp.maximum(i - 1, 0), 0)),
                  _const_spec(w.shape), _const_spec(b.shape), _const_spec(g.shape), _const_spec(beta.shape)],
        out_specs=pl.BlockSpec((1, rows, C), lambda b, i: (b, i, 0)),
        out_shape=jax.ShapeDtypeStruct((B, S, C), BF16),
        compiler_params=_params(("parallel", "arbitrary")),
        name="conv",
    )(hglu, hglu, w, b, g, beta)


def _out_proj_kernel(nsa_ref, conv_ref, x_ref, gin_ref, bin_ref, wa_ref, wb_ref, g_ref, b_ref, out_ref):
    h = _ln(x_ref[...], gin_ref[...], bin_ref[...])
    mix = _dot(nsa_ref[...], wa_ref[...]) + _dot(conv_ref[...], wb_ref[...])
    out_ref[...] = _ln(ALPHA * h + mix, g_ref[...], b_ref[...])


def _out_proj(o_nsa, o_conv, x2d, gin, bin_, wa, wb, g, b):
    T, D = x2d.shape
    tm = ROW_TILE
    row = lambda n: pl.BlockSpec((tm, n), lambda i: (i, 0))
    vec = _const_spec((1, D))
    return pl.pallas_call(
        _out_proj_kernel,
        grid=(T // tm,),
        in_specs=[row(o_nsa.shape[1]), row(o_conv.shape[1]), row(D), vec, vec,
                  _const_spec(wa.shape), _const_spec(wb.shape), vec, vec],
        out_specs=row(D),
        out_shape=jax.ShapeDtypeStruct((T, D), F32),
        compiler_params=_params(("parallel",)),
        name="out_proj",
    )(o_nsa, o_conv, x2d, gin, bin_, wa, wb, g, b)


def _mem_kv_kernel(m_ref, g_ref, b_ref, w_ref, out_ref):
    out_ref[...] = _dot(_ln(m_ref[...], g_ref[...], b_ref[...]).astype(BF16), w_ref[...]).astype(BF16)


def _mem_kv(mem2d, g, b, wkv):
    R, D = mem2d.shape
    N = wkv.shape[1]
    tn = 1024
    return pl.pallas_call(
        _mem_kv_kernel,
        grid=(R // ROW_TILE, N // tn),
        in_specs=[pl.BlockSpec((ROW_TILE, D), lambda i, j: (i, 0)), _const_spec((1, D)), _const_spec((1, D)),
                  pl.BlockSpec((D, tn), lambda i, j: (0, j))],
        out_specs=pl.BlockSpec((ROW_TILE, tn), lambda i, j: (i, j)),
        out_shape=jax.ShapeDtypeStruct((R, N), BF16),
        compiler_params=_params(("parallel", "parallel")),
        name="mem_kv",
    )(mem2d, g, b, wkv)


def _xattn_kernel(x_ref, kv_ref, wq_ref, wo_ref, g_ref, b_ref, rw_ref, rw_hi_ref, rb_ref,
                  x2_ref, x2p_ref, idx_ref, wt_ref):
    D = x_ref.shape[1]
    dh = D // X_HEADS
    x1 = x_ref[...]
    q = _dot(x1.astype(BF16), wq_ref[...]).astype(BF16)
    heads = []
    for h in range(X_HEADS):
        k = kv_ref[0, :, h * dh:(h + 1) * dh]
        v = kv_ref[0, :, D + h * dh:D + (h + 1) * dh]
        s = _dot_nt(q[:, h * dh:(h + 1) * dh], k) * (dh ** -0.5)
        e = jnp.exp(s - jnp.max(s, axis=-1, keepdims=True))
        p = e / jnp.sum(e, axis=-1, keepdims=True)
        heads.append(_dot(p.astype(BF16), v).astype(BF16))
    xa = _dot(jnp.concatenate(heads, axis=1), wo_ref[...])
    x2 = _ln(ALPHA * x1 + xa, g_ref[...], b_ref[...])
    x2_ref[...] = x2
    _rows_to_tiles(x2p_ref, _pack_halves(x2))

    x_hi = x2.astype(BF16)
    x_lo = (x2 - x_hi.astype(F32)).astype(BF16)
    both = _dot(x_hi, rw_ref[...])
    logit = both[:, 0:LANES] + both[:, LANES:2 * LANES] + _dot(x_lo, rw_hi_ref[...]) + rb_ref[...]
    lane = lax.broadcasted_iota(jnp.int32, logit.shape, 1)
    big = 3.0e38
    lg = jnp.where(lane < MOE_GROUPS, logit, -big)
    lg_max = jnp.max(lg, axis=-1, keepdims=True)
    grp = jnp.min(jnp.where(lg == lg_max, lane, LANES), axis=-1, keepdims=True)
    w_grp = 1.0 / jnp.sum(jnp.exp(lg - lg_max), axis=-1, keepdims=True)
    e_lane = lane - MOE_GROUPS
    in_grp = (e_lane >= grp * MOE_EPG) & (e_lane < (grp + 1) * MOE_EPG)
    le = jnp.where(in_grp, logit, -big)
    v1 = jnp.max(le, axis=-1, keepdims=True)
    i1 = jnp.min(jnp.where(le == v1, lane, LANES), axis=-1, keepdims=True)
    le2 = jnp.where(lane == i1, -big, le)
    v2 = jnp.max(le2, axis=-1, keepdims=True)
    i2 = jnp.min(jnp.where(le2 == v2, lane, LANES), axis=-1, keepdims=True)
    r = jnp.exp(v2 - v1)
    w1 = w_grp / (1.0 + r)
    w2 = w_grp * r / (1.0 + r)
    idx_ref[...] = jnp.where(lane == 0, i1 - MOE_GROUPS, jnp.where(lane == 1, i2 - MOE_GROUPS, 0))
    wt_ref[...] = jnp.where(lane == 0, w1, jnp.where(lane == 1, w2, 0.0))


def _xattn(x1, mem_kv, S, wq, wo, g, b, rw, rw_hi, rb):
    T, D = x1.shape
    tm = ROW_TILE
    row = lambda n: pl.BlockSpec((tm, n), lambda i: (i, 0))
    vec = _const_spec((1, D))
    per_step = S // tm
    return pl.pallas_call(
        _xattn_kernel,
        grid=(T // tm,),
        in_specs=[row(D), pl.BlockSpec((1,) + mem_kv.shape[1:], lambda i: (i // per_step, 0, 0)),
                  _const_spec(wq.shape), _const_spec(wo.shape), vec, vec,
                  _const_spec(rw.shape), _const_spec(rw_hi.shape), _const_spec(rb.shape)],
        out_specs=[row(D), pl.BlockSpec((tm * SUBLANES, LANES), lambda i: (i, 0)), row(LANES), row(LANES)],
        out_shape=[jax.ShapeDtypeStruct((T, D), F32), jax.ShapeDtypeStruct((T * SUBLANES, LANES), jnp.int32),
                   jax.ShapeDtypeStruct((T, LANES), jnp.int32), jax.ShapeDtypeStruct((T, LANES), F32)],
        compiler_params=_params(("parallel",)),
        name="xattn",
    )(x1, mem_kv, wq, wo, g, b, rw, rw_hi, rb)


def _tile(ref, t):
    return ref.at[pl.ds(pl.multiple_of(t * SUBLANES, SUBLANES), SUBLANES), :]


def _gather_copy(x_hbm, a, xbuf, sem, r):
    tok = jnp.maximum(a, 0) >> 1
    return pltpu.make_async_copy(_tile(x_hbm, tok), _tile(xbuf, r), sem)


def _scatter_copy(ybuf, a, y_hbm, sem, r, n_real, slot):
    dst = jnp.where(a >= 0, a, n_real + slot * MOE_ROWS + r)
    return pltpu.make_async_copy(_tile(ybuf, r), _tile(y_hbm, dst), sem)


def _moe_ffn_kernel(blk_e_ref, prev_ref, cur_ref, nxt_ref, x_hbm, wg_ref, wu_ref, wd_ref, y_hbm,
                    xbuf, ybuf, wg_sc, wu_sc, wd_sc, gsem, ssem):
    i = pl.program_id(0)

    @pl.when((i == 0) | (blk_e_ref[i] != blk_e_ref[jnp.maximum(i - 1, 0)]))
    def _():
        wg_sc[...] = wg_ref[0].astype(BF16)
        wu_sc[...] = wu_ref[0].astype(BF16)
        wd_sc[...] = wd_ref[0].astype(BF16)

    last = pl.num_programs(0) - 1
    slot = i % 2
    other = 1 - slot
    R = MOE_ROWS
    n_real = 2 * (x_hbm.shape[0] // SUBLANES)
    rows = range(R)
    gather = lambda ref, s, r: _gather_copy(x_hbm, ref[0, 0, r], xbuf.at[s], gsem.at[s], r)
    scatter = lambda a, s, r: _scatter_copy(ybuf.at[s], a, y_hbm, ssem.at[s], r, n_real, s)
    gather_wait = lambda s: pltpu.make_async_copy(xbuf.at[s], xbuf.at[s], gsem.at[s]).wait()
    scatter_wait = lambda s: pltpu.make_async_copy(ybuf.at[s], ybuf.at[s], ssem.at[s]).wait()

    @pl.when(i == 0)
    def _():
        ybuf[...] = jnp.zeros_like(ybuf)
        for r in rows:
            gather(cur_ref, 0, r).start()
            scatter(-1, 0, r).start()

    gather_wait(slot)
    scatter_wait(slot)
    for r in rows:
        gather(nxt_ref, other, r).start()
        scatter(jnp.where(i > 0, prev_ref[0, 0, r], -1), other, r).start()

    xb = _unpack_halves(_tiles_to_rows(xbuf.at[slot], R))
    hid = jax.nn.silu(_dot(xb, wg_sc[...])) * _dot(xb, wu_sc[...])
    _rows_to_tiles(ybuf.at[slot], _pack_halves(_dot(hid.astype(BF16), wd_sc[...])))

    @pl.when(i == last)
    def _():
        for r in rows:
            scatter(cur_ref[0, 0, r], slot, r).start()
        gather_wait(other)
        scatter_wait(other)
        scatter_wait(slot)


def _moe_ffn(a_buf, blk_e, x2p, w_gate, w_up, w_down):
    T = x2p.shape[0] // SUBLANES
    D = 2 * SUBLANES * LANES
    R = MOE_ROWS
    n_blk = a_buf.shape[0] // R
    Hd = w_gate.shape[2]
    a3 = a_buf.reshape(n_blk, 1, R)
    idx = lambda f: pl.BlockSpec((1, 1, R), lambda i, be: (f(i), 0, 0), memory_space=pltpu.SMEM)
    grid_spec = pltpu.PrefetchScalarGridSpec(
        num_scalar_prefetch=1,
        grid=(n_blk,),
        in_specs=[idx(lambda i: jnp.maximum(i - 1, 0)), idx(lambda i: i), idx(lambda i: jnp.minimum(i + 1, n_blk - 1)),
                  pl.BlockSpec(memory_space=pl.ANY),
                  pl.BlockSpec((1, D, Hd), lambda i, be: (be[i], 0, 0)),
                  pl.BlockSpec((1, D, Hd), lambda i, be: (be[i], 0, 0)),
                  pl.BlockSpec((1, Hd, D), lambda i, be: (be[i], 0, 0))],
        out_specs=pl.BlockSpec(memory_space=pl.ANY),
        scratch_shapes=[pltpu.VMEM((2, R * SUBLANES, LANES), jnp.int32), pltpu.VMEM((2, R * SUBLANES, LANES), jnp.int32),
                        pltpu.VMEM((D, Hd), BF16), pltpu.VMEM((D, Hd), BF16), pltpu.VMEM((Hd, D), BF16),
                        pltpu.SemaphoreType.DMA((2,)), pltpu.SemaphoreType.DMA((2,))],
    )
    return pl.pallas_call(
        _moe_ffn_kernel,
        grid_spec=grid_spec,
        out_shape=jax.ShapeDtypeStruct(((2 * T + 2 * R) * SUBLANES, LANES), jnp.int32),
        compiler_params=_params(("arbitrary",)),
        name="moe_ffn",
    )(blk_e, a3, a3, a3, x2p, w_gate, w_up, w_down)


def _combine_kernel(y_ref, wt_ref, x_ref, g_ref, b_ref, out_ref):
    rows = x_ref.shape[0]
    wt = wt_ref[...]
    y_slot = lambda k: _unpack_halves(_tiles_to_rows(y_ref, rows, k * SUBLANES, 2 * SUBLANES)).astype(F32)
    moe = wt[:, 0:1] * y_slot(0) + wt[:, 1:2] * y_slot(1)
    out_ref[...] = _ln(ALPHA * x_ref[...] + moe, g_ref[...], b_ref[...])


def _combine(y, wts, x2, g, b):
    T, D = x2.shape
    tm = ROW_TILE
    row = lambda n: pl.BlockSpec((tm, n), lambda i: (i, 0))
    return pl.pallas_call(
        _combine_kernel,
        grid=(T // tm,),
        in_specs=[pl.BlockSpec((2 * tm * SUBLANES, LANES), lambda i: (i, 0)), row(LANES), row(D),
                  _const_spec((1, D)), _const_spec((1, D))],
        out_specs=row(D),
        out_shape=jax.ShapeDtypeStruct((T, D), F32),
        compiler_params=_params(("parallel",)),
        name="moe_combine",
    )(y, wts, x2, g, b)


def _t5_bucket(dist):
    n = np.maximum(dist, 0)
    max_exact = REL_BUCKETS // 2
    nf = np.maximum(n, 1).astype(np.float64)
    large = max_exact + (np.log(nf / max_exact) / math.log(REL_MAX_DIST / max_exact)
                         * (REL_BUCKETS - max_exact)).astype(np.int64)
    large = np.minimum(large, REL_BUCKETS - 1)
    return np.where(n < max_exact, n, large)


def _bias_tiles(rel_table):
    Hk, G, QB, NB = NSA_KV_HEADS, NSA_GROUP, Q_BLOCK, STEP_BLOCKS
    tab = rel_table.reshape(REL_BUCKETS, Hk, G).transpose(1, 2, 0) * LOG2E
    far = tab[:, :, REL_BUCKETS - 1]
    far_hi = far.astype(BF16).astype(F32)
    far_lo = far - far_hi
    rel = tab - far[:, :, None]

    def sub_tile(dist, valid):
        bucket = _t5_bucket(dist)
        if not (valid & (bucket != REL_BUCKETS - 1)).any():
            return jnp.broadcast_to(jnp.asarray(np.where(valid, 0.0, NEG), F32), (Hk, G) + dist.shape)
        onehot = jnp.asarray(bucket[None] == np.arange(REL_BUCKETS)[:, None, None], F32)
        bias = jnp.einsum('hgk,kqn->hgqn', rel, onehot, precision=lax.Precision.HIGHEST)
        return jnp.where(jnp.asarray(valid), bias, NEG)

    def tile(dist_fn, valid_fn, n_keys):
        qr = np.arange(QB)[:, None]
        kk = np.arange(LANES)[None, :]
        blocks = []
        for b in range(NB):
            subs = []
            for c in range(n_keys // LANES):
                dist = dist_fn(b, qr, kk + c * LANES)
                subs.append(sub_tile(dist, valid_fn(dist)))
            blocks.append(jnp.concatenate(subs, axis=-1))
        return jnp.stack(blocks, axis=1).reshape(Hk, NB * G * QB, n_keys).astype(F32)

    wt = tile(lambda b, qr, kr: WIN + QB * b + qr - kr, lambda d: (d >= 0) & (d < WIN), WIN + STEP_Q)
    nt = tile(lambda b, qr, kr: STEP_Q + QB * b + qr - kr, lambda d: d >= 0, 2 * STEP_Q)
    frame_lo = CMP_FRAME - LANES
    ct = tile(lambda b, qr, r: (QB * (b - NB) + qr + CMP_STRIDE * (CMP_FRAME - frame_lo - r) - (CMP_BLOCK - 1)),
              lambda d: d >= 0, LANES)
    lane = np.arange(LANES)[None, None, None, :]
    qx = (jnp.where(lane == COL_ONE, far_hi[:, :, None, None], 0.0)
          + jnp.where(lane == COL_ONE2, far_lo[:, :, None, None], 0.0)
          + jnp.where(lane == COL_PAD, NEG, 0.0))
    qx = jnp.broadcast_to(qx[:, None], (Hk, NB, G, QB, LANES)).reshape(Hk, NB * G * QB, LANES)
    return wt, nt, ct, qx.astype(F32)


def _static_tables():
    G = NSA_GROUP
    rows = np.arange(SEL_FRAME * SEL_BLOCK)
    eye = (rows[:, None] // SEL_BLOCK == np.arange(SEL_FRAME)[None, :]).astype(np.float32)
    eye = np.concatenate([np.zeros((KEY_PAD, SEL_FRAME), np.float32), eye], axis=0)
    cs = np.arange(CMP_FRAME) * CMP_STRIDE
    ss = np.arange(SEL_FRAME) * SEL_BLOCK
    ov = np.clip(np.minimum(cs[:, None] + CMP_BLOCK, ss[None, :] + SEL_BLOCK)
                 - np.maximum(cs[:, None], ss[None, :]), 0, None) / CMP_BLOCK
    pick = np.zeros((G, G * HEAD_DIM, LANES), np.float32)
    place = np.zeros((G, LANES, G * HEAD_DIM), np.float32)
    for g in range(G):
        for d in range(HEAD_DIM):
            pick[g, g * HEAD_DIM + d, d] = 1.0
            place[g, d, g * HEAD_DIM + d] = 1.0
    return (jnp.asarray(eye, BF16), jnp.asarray(ov.T, BF16), jnp.asarray(pick, BF16), jnp.asarray(place, BF16))


def _dispatch(idx, T):
    R, E = MOE_ROWS, MOE_EXPERTS
    A = 2 * T
    id_bits = max(A - 1, 1).bit_length()
    e_flat = idx[:, 0:2].reshape(A)
    counts = jnp.sum((e_flat[:, None] == jnp.arange(E, dtype=jnp.int32)[None, :]).astype(jnp.int32), axis=0)
    n_pad = (-counts) % R
    real = (e_flat << (id_bits + 1)) | jnp.arange(A, dtype=jnp.int32)
    d_e = jnp.arange(E, dtype=jnp.int32)[:, None]
    d_r = jnp.arange(R, dtype=jnp.int32)[None, :]
    pad = jnp.where(d_r < n_pad[:, None], (d_e << (id_bits + 1)) | (1 << id_bits) | d_r,
                    (E << (id_bits + 1)) | (d_e * R + d_r))
    keys = jnp.sort(jnp.concatenate([real, pad.reshape(E * R)]))
    expert = keys >> (id_bits + 1)
    is_pad = ((keys >> id_bits) & 1) == 1
    a_buf = jnp.where(is_pad | (expert >= E), -1, keys & ((1 << id_bits) - 1))
    blk_e = jnp.minimum(expert.reshape(-1, R)[:, 0], E - 1)
    return a_buf, blk_e


def _forward(x, mem, ln_in_g, ln_in_b, w_in, cmp_pe_k, cmp_w1_k, cmp_w2_k, cmp_pe_v, cmp_w1_v, cmp_w2_v, rel_table, conv_dw_w, conv_dw_b, conv_ln_g, conv_ln_b, w_out, ln1_g, ln1_b, mem_ln_g, mem_ln_b, xa_wq, xa_wkv, xa_wo, ln2_g, ln2_b, router_group_w, router_group_b, router_expert_w, router_expert_b, moe_w_gate, moe_w_up, moe_w_down, ln3_g, ln3_b):
    B, S, D = x.shape
    T = B * S
    Hk, G, dh = NSA_KV_HEADS, NSA_GROUP, HEAD_DIM
    assert S % ROW_TILE == 0 and S % STEP_Q == 0 and SEL_TOPK * SEL_BLOCK <= S <= SEL_FRAME * SEL_BLOCK
    vec = lambda v: v.reshape(1, -1).astype(F32)
    x2d = x.reshape(T, D)

    w = w_in[0]
    nq, nkv = NSA_HEADS * dh, Hk * dh
    c_g = nq + 6 * nkv
    c_u = c_g + NSA_HEADS * 3
    cw = (w.shape[1] - c_u) // 2
    wq = w[:, :nq].astype(BF16)
    wkv = w[:, nq:c_g].astype(BF16)
    wg = jnp.pad(w[:, c_g:c_u].reshape(D, Hk, G * 3), ((0, 0), (0, 0), (0, LANES - G * 3)))
    wg = wg.reshape(D, Hk * LANES).astype(BF16)
    wa = w[:, c_u:c_u + cw].astype(BF16)
    wgt = w[:, c_u + cw:].astype(BF16)
    q, kc_in, vc_in, ks, vs, kw, vw, gates, hglu = _in_proj(x2d, S, vec(ln_in_g), vec(ln_in_b),
                                                             wq, wkv, wg, wa, wgt)

    def compressed(t, pe, w1, w2, ones_cols, pad_col):
        ch = t.reshape(B, Hk, S // CMP_STRIDE, CMP_STRIDE * dh)
        half = CMP_STRIDE * dh
        pe8 = jnp.pad(pe.reshape(1, CMP_BLOCK * dh), ((0, 7), (0, 0))).astype(BF16)
        w2p = jnp.pad(w2, ((0, 0), (0, LANES - dh))).astype(BF16)
        lane = np.arange(LANES)
        tail = np.zeros((8, LANES), np.float32)
        tail[0] = np.isin(lane, ones_cols)
        if pad_col is not None:
            tail[1, pad_col] = 1.0
        w1b16 = w1.astype(BF16)
        return _compress(ch, pe8, w1b16[:half], w1b16[half:], w1b16, w2p, jnp.asarray(tail))

    kc = compressed(kc_in, cmp_pe_k[0], cmp_w1_k[0], cmp_w2_k[0], (COL_ONE, COL_ONE2), COL_PAD)
    vc = compressed(vc_in, cmp_pe_v[0], cmp_w1_v[0], cmp_w2_v[0], (), None)

    eye, ovt, pick, place = _static_tables()
    wt, nt, ct, qx = _bias_tiles(rel_table.astype(F32))
    o_nsa = _nsa(q.reshape(B, S, nq), gates.reshape(B, S, Hk * LANES), kc, vc, ks, vs, kw, vw,
                 eye, ovt, wt, nt, ct, qx, pick, place).reshape(T, nq)

    o_conv = _conv(hglu.reshape(B, S, cw), conv_dw_w[0].reshape(CONV_K, cw), vec(conv_dw_b[0]),
                   vec(conv_ln_g[0]), vec(conv_ln_b[0])).reshape(T, cw)

    wo = w_out[0].astype(BF16)
    x1 = _out_proj(o_nsa, o_conv, x2d, vec(ln_in_g), vec(ln_in_b), wo[:nq], wo[nq:], vec(ln1_g[0]), vec(ln1_b[0]))

    mkv = _mem_kv(mem.reshape(-1, D), vec(mem_ln_g[0]), vec(mem_ln_b[0]), xa_wkv[0].astype(BF16))
    mkv = mkv.reshape(B, mem.shape[1], 2 * D)
    rw = jnp.concatenate([router_group_w[0], router_expert_w[0]], axis=1).astype(F32)
    rw = jnp.pad(rw, ((0, 0), (0, LANES - rw.shape[1])))
    rb = jnp.pad(jnp.concatenate([router_group_b[0], router_expert_b[0]]).astype(F32), (0, LANES - MOE_GROUPS - MOE_EXPERTS))
    rw_hi = rw.astype(BF16)
    rw_lo = (rw - rw_hi.astype(F32)).astype(BF16)
    x2, x2p, idx, wts = _xattn(x1, mkv, S, xa_wq[0].astype(BF16), xa_wo[0].astype(BF16), vec(ln2_g[0]), vec(ln2_b[0]),
                          jnp.concatenate([rw_hi, rw_lo], axis=1), rw_hi, rb.reshape(1, LANES))

    a_buf, blk_e = _dispatch(idx, T)
    y = _moe_ffn(a_buf, blk_e, x2p, moe_w_gate[0], moe_w_up[0], moe_w_down[0])
    out = _combine(y, wts, x2, vec(ln3_g[0]), vec(ln3_b[0]))
    stages = dict(q=q, kc=kc, vc=vc, o_nsa=o_nsa, o_conv=o_conv, x1=x1, x2=x2)
    return out.reshape(B, S, D), stages


def kernel(x, mem, ln_in_g, ln_in_b, w_in, cmp_pe_k, cmp_w1_k, cmp_w2_k, cmp_pe_v, cmp_w1_v, cmp_w2_v, rel_table, conv_dw_w, conv_dw_b, conv_ln_g, conv_ln_b, w_out, ln1_g, ln1_b, mem_ln_g, mem_ln_b, xa_wq, xa_wkv, xa_wo, ln2_g, ln2_b, router_group_w, router_group_b, router_expert_w, router_expert_b, moe_w_gate, moe_w_up, moe_w_down, ln3_g, ln3_b):
    out, _ = _forward(x, mem, ln_in_g, ln_in_b, w_in, cmp_pe_k, cmp_w1_k, cmp_w2_k, cmp_pe_v, cmp_w1_v, cmp_w2_v, rel_table, conv_dw_w, conv_dw_b, conv_ln_g, conv_ln_b, w_out, ln1_g, ln1_b, mem_ln_g, mem_ln_b, xa_wq, xa_wkv, xa_wo, ln2_g, ln2_b, router_group_w, router_group_b, router_expert_w, router_expert_b, moe_w_gate, moe_w_up, moe_w_down, ln3_g, ln3_b)
    return out
```

```python
import functools
import math

import jax
import jax.numpy as jnp
import numpy as np
from jax import lax
from jax.experimental import pallas as pl
from jax.experimental.pallas import tpu as pltpu

HEAD_DIM = 64
NSA_HEADS = 16
NSA_KV_HEADS = 4
NSA_GROUP = 4
CMP_BLOCK = 32
CMP_STRIDE = 16
CMP_HIDDEN = 256
SEL_BLOCK = 64
SEL_TOPK = 16
WIN = 512
Q_BLOCK = 128
CONV_K = 31
REL_BUCKETS = 32
REL_MAX_DIST = 128
X_HEADS = 4
MOE_GROUPS = 4
MOE_EPG = 8
MOE_EXPERTS = 32
LN_EPS = 1e-5
DEPTH = 1
ALPHA = (2 * DEPTH) ** 0.25

LANES = 128
SUBLANES = 8
VMEM_LIMIT = 56 * 1024 * 1024
ROW_TILE = 256
MOE_ROWS = 256

STEP_BLOCKS = 2
STEP_Q = STEP_BLOCKS * Q_BLOCK
CMP_FRAME = 512
SEL_FRAME = 128
KEY_PAD = 512
FAR_CHUNK = 512
NEG = -float(2 ** 30)
M_FLOOR = -float(2 ** 27)
LOG2E = math.log2(math.e)
COL_ONE, COL_PAD, COL_ONE2 = 64, 65, 66

F32 = jnp.float32
BF16 = jnp.bfloat16


def _dot(a, b):
    return jnp.dot(a, b, preferred_element_type=F32)


def _dot_nt(a, b):
    return lax.dot_general(a, b, (((1,), (1,)), ((), ())), preferred_element_type=F32)


def _ln(x, g, b):
    mu = jnp.mean(x, axis=-1, keepdims=True)
    xc = x - mu
    var = jnp.mean(xc * xc, axis=-1, keepdims=True)
    return xc * lax.rsqrt(var + LN_EPS) * g + b


def _pack_halves(x):
    n = x.shape[1] // 2
    bits = lambda v: lax.bitcast_convert_type(v.astype(BF16).astype(F32), jnp.uint32)
    return lax.bitcast_convert_type((bits(x[:, :n]) >> 16) | bits(x[:, n:]), jnp.int32)


def _unpack_halves(w):
    u = lax.bitcast_convert_type(w, jnp.uint32)
    lo = lax.bitcast_convert_type(u << 16, F32).astype(BF16)
    hi = lax.bitcast_convert_type(u & jnp.uint32(0xFFFF0000), F32).astype(BF16)
    return jnp.concatenate([lo, hi], axis=1)


def _rows_to_tiles(ref, packed):
    rows = packed.shape[0]
    for s in range(SUBLANES):
        ref[pl.ds(s, rows, stride=SUBLANES), :] = packed[:, s * LANES:(s + 1) * LANES]


def _tiles_to_rows(ref, rows, first=0, step=SUBLANES):
    return jnp.concatenate([ref[pl.ds(first + s, rows, stride=step), :] for s in range(SUBLANES)], axis=1)


def _params(sem):
    return pltpu.CompilerParams(dimension_semantics=sem, vmem_limit_bytes=VMEM_LIMIT)


def _const_spec(shape):
    nd = len(shape)
    return pl.BlockSpec(shape, lambda *_: (0,) * nd)


def _in_proj_kernel(x_ref, g_ref, b_ref, wq_ref, wkv_ref, wg_ref, wa_ref, wgt_ref, ks_in, vs_in, kw_in, vw_in,
                    q_ref, kc_ref, vc_ref, ks_ref, vs_ref, kw_ref, vw_ref, gate_ref, glu_ref):
    del ks_in, vs_in, kw_in, vw_in
    Hk, dh = NSA_KV_HEADS, HEAD_DIM
    h = _ln(x_ref[...], g_ref[...], b_ref[...]).astype(BF16)
    q_ref[...] = (_dot(h, wq_ref[...]) * (dh ** -0.5 * LOG2E)).astype(BF16)
    kv = _dot(h, wkv_ref[...])
    for c, ref in enumerate((kc_ref, vc_ref)):
        for hh in range(Hk):
            ref[0, hh] = kv[:, (c * Hk + hh) * dh:(c * Hk + hh + 1) * dh].astype(BF16)
    lane = lax.broadcasted_iota(jnp.int32, (kv.shape[0], LANES), 1)
    k_extra = jnp.where((lane == COL_ONE) | (lane == COL_ONE2), 1.0, 0.0)
    v_extra = jnp.where(lane == COL_ONE, 1.0, 0.0)
    for c, ref, extra in ((2, ks_ref, k_extra), (3, vs_ref, v_extra), (4, kw_ref, k_extra), (5, vw_ref, v_extra)):
        for pair in range(Hk // 2):
            both = kv[:, (c * Hk + 2 * pair) * dh:(c * Hk + 2 * pair + 2) * dh]
            ref[0, 2 * pair] = jnp.where(lane < dh, both, extra).astype(BF16)
            ref[0, 2 * pair + 1] = jnp.where(lane < dh, pltpu.roll(both, dh, 1), extra).astype(BF16)
    gate_ref[...] = jax.nn.sigmoid(_dot(h, wg_ref[...]))
    glu_ref[...] = _dot(h, wa_ref[...]) * jax.nn.sigmoid(_dot(h, wgt_ref[...]))


def _key_template(shape, pad_col):
    pad_row = np.zeros((LANES,), np.float32)
    if pad_col is not None:
        pad_row[pad_col] = 1.0
    front = jnp.broadcast_to(jnp.asarray(pad_row, shape.dtype), shape.shape[:2] + (KEY_PAD, LANES))
    body = jnp.zeros(shape.shape[:2] + (shape.shape[2] - KEY_PAD, LANES), shape.dtype)
    return jnp.concatenate([front, body], axis=2)


def _in_proj(x2d, S, g, b, wq, wkv, wg, wa, wgt):
    T, D = x2d.shape
    tm = ROW_TILE
    row = lambda n: pl.BlockSpec((tm, n), lambda i: (i, 0))
    per_seq = S // tm
    B, Hk = T // S, NSA_KV_HEADS
    assert KEY_PAD % tm == 0
    cmp_spec = pl.BlockSpec((1, Hk, tm, HEAD_DIM), lambda i: (i // per_seq, 0, i % per_seq, 0))
    key_spec = pl.BlockSpec((1, Hk, tm, LANES), lambda i: (i // per_seq, 0, i % per_seq + KEY_PAD // tm, 0))
    cmp_shape = jax.ShapeDtypeStruct((B, Hk, S, HEAD_DIM), BF16)
    key_shape = jax.ShapeDtypeStruct((B, Hk, KEY_PAD + S, LANES), BF16)
    return pl.pallas_call(
        _in_proj_kernel,
        grid=(T // tm,),
        in_specs=[row(D), _const_spec((1, D)), _const_spec((1, D)), _const_spec(wq.shape),
                  _const_spec(wkv.shape), _const_spec(wg.shape), _const_spec(wa.shape),
                  _const_spec(wgt.shape)] + [pl.BlockSpec(memory_space=pl.ANY)] * 4,
        input_output_aliases={8: 3, 9: 4, 10: 5, 11: 6},
        out_specs=[row(wq.shape[1]), cmp_spec, cmp_spec, key_spec, key_spec, key_spec, key_spec,
                   row(wg.shape[1]), row(wa.shape[1])],
        out_shape=[jax.ShapeDtypeStruct((T, wq.shape[1]), BF16),
                   cmp_shape, cmp_shape, key_shape, key_shape, key_shape, key_shape,
                   jax.ShapeDtypeStruct((T, wg.shape[1]), F32),
                   jax.ShapeDtypeStruct((T, wa.shape[1]), F32)],
        compiler_params=_params(("parallel",)),
        name="in_proj",
    )(x2d, g, b, wq, wkv, wg, wa, wgt, _key_template(key_shape, COL_PAD), _key_template(key_shape, None),
      _key_template(key_shape, COL_PAD), _key_template(key_shape, None))


def _compress_kernel(ch_ref, pe_ref, w1a_ref, w1b_ref, w1_ref, w2_ref, tail_ref, out_ref, *, n_rows):
    ch = ch_ref[0, 0]
    first = _dot(ch, w1a_ref[...])
    second = _dot(ch, w1b_ref[...])
    second = pltpu.roll(second, n_rows - 1, 0)
    pe_term = _dot(pe_ref[...], w1_ref[...])[0:1, :]
    hid = jax.nn.gelu(first + second + pe_term).astype(BF16)
    res = _dot(hid, w2_ref[...]) + tail_ref[0:1, :]
    out_ref[0, 0, 0:CMP_FRAME, :] = jnp.broadcast_to(tail_ref[1:2, :], (CMP_FRAME, LANES))
    out_ref[0, 0, CMP_FRAME:CMP_FRAME + n_rows, :] = res


def _compress(ch, pe8, w1a, w1b, w1, w2p, tail):
    B, Hk, n_rows, width = ch.shape
    kern = functools.partial(_compress_kernel, n_rows=n_rows)
    return pl.pallas_call(
        kern,
        grid=(B, Hk),
        in_specs=[pl.BlockSpec((1, 1, n_rows, width), lambda b, h: (b, h, 0, 0)),
                  _const_spec(pe8.shape), _const_spec(w1a.shape), _const_spec(w1b.shape),
                  _const_spec(w1.shape), _const_spec(w2p.shape), _const_spec(tail.shape)],
        out_specs=pl.BlockSpec((1, 1, CMP_FRAME + n_rows, LANES), lambda b, h: (b, h, 0, 0)),
        out_shape=jax.ShapeDtypeStruct((B, Hk, CMP_FRAME + n_rows, LANES), F32),
        compiler_params=_params(("parallel", "parallel")),
        name="compress",
    )(ch, pe8, w1a, w1b, w1, w2p, tail)


def _softmax_update(carry, s, v):
    m, acc = carry
    m_new = jnp.maximum(m, jnp.max(s, axis=-1, keepdims=True))
    p = jnp.exp2(s - m_new).astype(BF16)
    acc = acc * jnp.exp2(m - m_new) + _dot(p, v)
    return m_new, acc


def _softmax_merge(a, b):
    m = jnp.maximum(a[0], b[0])
    return m, a[1] * jnp.exp2(a[0] - m) + b[1] * jnp.exp2(b[0] - m)


def _add_on_lanes(s, lo, tile):
    hi = lo + tile.shape[1]
    parts = [s[:, :lo]] if lo else []
    parts.append(s[:, lo:hi] + tile)
    if hi < s.shape[1]:
        parts.append(s[:, hi:])
    return jnp.concatenate(parts, axis=1)


def _inv_row_sum(acc):
    return 1.0 / jnp.maximum(acc[:, COL_ONE:COL_ONE + 1], 1e-30)


def _nsa_kernel(q_ref, gate_ref, kc_ref, vc_ref, ks_ref, vs_ref, kw_ref, vw_ref,
                eye_ref, ovt_ref, wt_ref, nt_ref, ct_ref, qx_ref, pick_ref, place_ref, out_ref):
    si = pl.program_id(2)
    NB, G, QB = STEP_BLOCKS, NSA_GROUP, Q_BLOCK
    rows = NB * G * QB
    groups = [(b, g) for b in range(NB) for g in range(G)]
    row_slice = lambda b, g: slice((b * G + g) * QB, (b * G + g + 1) * QB)
    init = (jnp.full((rows, 1), M_FLOOR, F32), jnp.zeros((rows, LANES), F32))

    qblk = q_ref[0]
    q_pad = jnp.concatenate([_dot(qblk[b * QB:(b + 1) * QB], pick_ref[g]) for b, g in groups], axis=0)
    q_pad = (q_pad + qx_ref[0]).astype(BF16)
    q0 = si * STEP_Q

    c0 = pl.multiple_of(SUBLANES * NB * (si + 1), SUBLANES)
    kc = kc_ref[0, 0, pl.ds(c0, CMP_FRAME), :].astype(BF16)
    vc = vc_ref[0, 0, pl.ds(c0, CMP_FRAME), :].astype(BF16)
    s_c = _add_on_lanes(_dot_nt(q_pad, kc), CMP_FRAME - LANES, ct_ref[0])
    m_c = jnp.maximum(jnp.max(s_c, axis=-1, keepdims=True), M_FLOOR)
    e_c = jnp.exp2(s_c - m_c)
    p_c = e_c * (1.0 / jnp.maximum(jnp.sum(e_c, axis=-1, keepdims=True), 1e-30))
    o_c = _dot(p_c.astype(BF16), vc)

    p_sum = jnp.concatenate([sum(p_c[row_slice(b, g)] for g in range(G)) for b in range(NB)], axis=0)
    p_hi = p_sum.astype(BF16)
    p_lo = (p_sum - p_hi.astype(F32)).astype(BF16)
    imp_t = _dot_nt(ovt_ref[...], p_hi) + _dot_nt(ovt_ref[...], p_lo)
    jr = lax.broadcasted_iota(jnp.int32, (SEL_FRAME, STEP_Q), 0)
    ql = lax.broadcasted_iota(jnp.int32, (SEL_FRAME, STEP_Q), 1)
    j_abs = jr + (2 * NB * (si + 1) - SEL_FRAME)
    cur = (SEL_FRAME - 2 * NB) + ql // SEL_BLOCK
    exists = j_abs >= 0
    forced = exists & ((j_abs == 0) | (jr == cur) | (jr == cur - 1))
    valid = exists & (jr <= cur)
    free = valid & jnp.logical_not(forced)
    big = 3.0e38
    score = jnp.where(free, imp_t, -big)
    chosen = jnp.zeros((SEL_FRAME, STEP_Q), jnp.bool_)
    for _ in range(SEL_TOPK - 3):
        top = jnp.max(score, axis=0, keepdims=True)
        first = jnp.min(jnp.where(score == top, jr, SEL_FRAME), axis=0, keepdims=True)
        hit = jr == first
        chosen = chosen | hit
        score = jnp.where(hit, -big, score)
    neg_t = jnp.where(forced | (chosen & free), 0.0, NEG)
    neg = neg_t.T.astype(BF16)
    neg_rows = jnp.concatenate([neg[b * QB:(b + 1) * QB] for b, _ in groups], axis=0)
    q_sel = jnp.concatenate([neg_rows, q_pad], axis=1)

    r_win = q0 - WIN + KEY_PAD
    carry_w = init
    for lo in range(0, WIN + STEP_Q, FAR_CHUNK):
        n = min(FAR_CHUNK, WIN + STEP_Q - lo)
        r = pl.multiple_of(r_win + lo, LANES)
        s_w = _dot_nt(q_pad, kw_ref[0, 0, pl.ds(r, n), :]) + wt_ref[0, :, lo:lo + n]
        carry_w = _softmax_update(carry_w, s_w, vw_ref[0, 0, pl.ds(r, n), :])
    acc_w = carry_w[1]

    near_blocks = 2 * STEP_Q // SEL_BLOCK
    far_end = q0 - STEP_Q + KEY_PAD
    n_far = (jnp.maximum(q0 - STEP_Q, 0) + FAR_CHUNK - 1) // FAR_CHUNK

    def far_chunk(chain, i, live):
        r0 = pl.multiple_of(jnp.where(live, far_end - FAR_CHUNK * (i + 1), 0), LANES)
        e0 = jnp.where(live, SEL_BLOCK * (SEL_FRAME - near_blocks) - FAR_CHUNK * (i + 1) + KEY_PAD, 0)
        e0 = pl.multiple_of(e0, LANES)
        keys = jnp.concatenate([eye_ref[pl.ds(e0, FAR_CHUNK), :], ks_ref[0, 0, pl.ds(r0, FAR_CHUNK), :]], axis=1)
        s = _dot_nt(q_sel, keys)
        return _softmax_update(chain, s, vs_ref[0, 0, pl.ds(r0, FAR_CHUNK), :])

    def far_pair(i, carry):
        return far_chunk(far_chunk(carry, 2 * i, True), 2 * i + 1, 2 * i + 1 < n_far)

    r_near = pl.multiple_of(far_end, LANES)
    e_near = SEL_BLOCK * (SEL_FRAME - near_blocks) + KEY_PAD
    keys = jnp.concatenate([eye_ref[e_near:e_near + 2 * STEP_Q, :], ks_ref[0, 0, pl.ds(r_near, 2 * STEP_Q), :]],
                           axis=1)
    s_n = _dot_nt(q_sel, keys) + nt_ref[0]
    near = _softmax_update(init, s_n, vs_ref[0, 0, pl.ds(r_near, 2 * STEP_Q), :])
    _, acc_s = lax.fori_loop(0, (n_far + 1) // 2, far_pair, near)

    gates = gate_ref[0]
    def gate_col(br):
        return jnp.concatenate([gates[b * QB:(b + 1) * QB, 3 * g + br:3 * g + br + 1] for b, g in groups], axis=0)
    o = (gate_col(0) * o_c + (gate_col(1) * _inv_row_sum(acc_s)) * acc_s
         + (gate_col(2) * _inv_row_sum(acc_w)) * acc_w).astype(BF16)
    out = [sum(_dot(o[row_slice(b, g)], place_ref[g]) for g in range(G)) for b in range(NB)]
    out_ref[0] = jnp.concatenate(out, axis=0).astype(BF16)


def _nsa(q, gates, kc, vc, ks, vs, kw, vw, eye, ovt, wt, nt, ct, qx, pick, place):
    B, S, _ = q.shape
    Hk = NSA_KV_HEADS
    per_head = lambda a: pl.BlockSpec((1, 1) + a.shape[2:], lambda b, h, i: (b, h, 0, 0))
    tab = lambda a: pl.BlockSpec((1,) + a.shape[1:], lambda b, h, i: (h, 0, 0))
    qspec = pl.BlockSpec((1, STEP_Q, NSA_GROUP * HEAD_DIM), lambda b, h, i: (b, i, h))
    return pl.pallas_call(
        _nsa_kernel,
        grid=(B, Hk, S // STEP_Q),
        in_specs=[qspec, pl.BlockSpec((1, STEP_Q, LANES), lambda b, h, i: (b, i, h)),
                  per_head(kc), per_head(vc), per_head(ks), per_head(vs), per_head(kw), per_head(vw),
                  _const_spec(eye.shape), _const_spec(ovt.shape), tab(wt), tab(nt), tab(ct), tab(qx),
                  _const_spec(pick.shape), _const_spec(place.shape)],
        out_specs=qspec,
        out_shape=jax.ShapeDtypeStruct(q.shape, BF16),
        compiler_params=_params(("parallel", "parallel", "arbitrary")),
        name="nsa",
    )(q, gates, kc, vc, ks, vs, kw, vw, eye, ovt, wt, nt, ct, qx, pick, place)


CONV_HALO = 32


def _conv_kernel(cur_ref, prev_ref, w_ref, b_ref, g_ref, beta_ref, out_ref):
    i = pl.program_id(1)
    cur = cur_ref[0]
    rows = cur.shape[0]
    halo = prev_ref[0, rows - CONV_HALO:rows, :]
    halo = jnp.where(i > 0, halo, 0.0)
    ext = jnp.concatenate([halo, cur], axis=0)
    acc = b_ref[...]
    for s in range(SUBLANES):
        part = 0.0
        for a in range((CONV_K - 1 - s) // SUBLANES + 1):
            lo = CONV_HALO - SUBLANES * (a + 1)
            k = CONV_K - 1 - SUBLANES * a - s
            part = part + ext[lo:lo + rows + SUBLANES, :] * w_ref[k:k + 1, :]
        acc = acc + part[SUBLANES - s:SUBLANES - s + rows, :]
    y = _ln(acc, g_ref[...], beta_ref[...])
    out_ref[0] = (y * jax.nn.sigmoid(y)).astype(BF16)


def _conv(hglu, w, b, g, beta):
    B, S, C = hglu.shape
    rows = ROW_TILE
    return pl.pallas_call(
        _conv_kernel,
        grid=(B, S // rows),
        in_specs=[pl.BlockSpec((1, rows, C), lambda b, i: (b, i, 0)),
                  pl.BlockSpec((1, rows, C), lambda b, i: (b, jnp.maximum(i - 1, 0), 0)),
                  _const_spec(w.shape), _const_spec(b.shape), _const_spec(g.shape), _const_spec(beta.shape)],
        out_specs=pl.BlockSpec((1, rows, C), lambda b, i: (b, i, 0)),
        out_shape=jax.ShapeDtypeStruct((B, S, C), BF16),
        compiler_params=_params(("parallel", "arbitrary")),
        name="conv",
    )(hglu, hglu, w, b, g, beta)


def _out_proj_kernel(nsa_ref, conv_ref, x_ref, gin_ref, bin_ref, wa_ref, wb_ref, g_ref, b_ref, out_ref):
    h = _ln(x_ref[...], gin_ref[...], bin_ref[...])
    mix = _dot(nsa_ref[...], wa_ref[...]) + _dot(conv_ref[...], wb_ref[...])
    out_ref[...] = _ln(ALPHA * h + mix, g_ref[...], b_ref[...])


def _out_proj(o_nsa, o_conv, x2d, gin, bin_, wa, wb, g, b):
    T, D = x2d.shape
    tm = ROW_TILE
    row = lambda n: pl.BlockSpec((tm, n), lambda i: (i, 0))
    vec = _const_spec((1, D))
    return pl.pallas_call(
        _out_proj_kernel,
        grid=(T // tm,),
        in_specs=[row(o_nsa.shape[1]), row(o_conv.shape[1]), row(D), vec, vec,
                  _const_spec(wa.shape), _const_spec(wb.shape), vec, vec],
        out_specs=row(D),
        out_shape=jax.ShapeDtypeStruct((T, D), F32),
        compiler_params=_params(("parallel",)),
        name="out_proj",
    )(o_nsa, o_conv, x2d, gin, bin_, wa, wb, g, b)


def _mem_kv_kernel(m_ref, g_ref, b_ref, w_ref, out_ref):
    out_ref[...] = _dot(_ln(m_ref[...], g_ref[...], b_ref[...]).astype(BF16), w_ref[...]).astype(BF16)


def _mem_kv(mem2d, g, b, wkv):
    R, D = mem2d.shape
    N = wkv.shape[1]
    tn = 1024
    return pl.pallas_call(
        _mem_kv_kernel,
        grid=(R // ROW_TILE, N // tn),
        in_specs=[pl.BlockSpec((ROW_TILE, D), lambda i, j: (i, 0)), _const_spec((1, D)), _const_spec((1, D)),
                  pl.BlockSpec((D, tn), lambda i, j: (0, j))],
        out_specs=pl.BlockSpec((ROW_TILE, tn), lambda i, j: (i, j)),
        out_shape=jax.ShapeDtypeStruct((R, N), BF16),
        compiler_params=_params(("parallel", "parallel")),
        name="mem_kv",
    )(mem2d, g, b, wkv)


def _xattn_kernel(x_ref, kv_ref, wq_ref, wo_ref, g_ref, b_ref, rw_ref, rw_hi_ref, rb_ref,
                  x2_ref, x2p_ref, idx_ref, wt_ref):
    D = x_ref.shape[1]
    dh = D // X_HEADS
    x1 = x_ref[...]
    q = _dot(x1.astype(BF16), wq_ref[...]).astype(BF16)
    heads = []
    for h in range(X_HEADS):
        k = kv_ref[0, :, h * dh:(h + 1) * dh]
        v = kv_ref[0, :, D + h * dh:D + (h + 1) * dh]
        s = _dot_nt(q[:, h * dh:(h + 1) * dh], k) * (dh ** -0.5)
        e = jnp.exp(s - jnp.max(s, axis=-1, keepdims=True))
        p = e / jnp.sum(e, axis=-1, keepdims=True)
        heads.append(_dot(p.astype(BF16), v).astype(BF16))
    xa = _dot(jnp.concatenate(heads, axis=1), wo_ref[...])
    x2 = _ln(ALPHA * x1 + xa, g_ref[...], b_ref[...])
    x2_ref[...] = x2
    _rows_to_tiles(x2p_ref, _pack_halves(x2))

    x_hi = x2.astype(BF16)
    x_lo = (x2 - x_hi.astype(F32)).astype(BF16)
    both = _dot(x_hi, rw_ref[...])
    logit = both[:, 0:LANES] + both[:, LANES:2 * LANES] + _dot(x_lo, rw_hi_ref[...]) + rb_ref[...]
    lane = lax.broadcasted_iota(jnp.int32, logit.shape, 1)
    big = 3.0e38
    lg = jnp.where(lane < MOE_GROUPS, logit, -big)
    lg_max = jnp.max(lg, axis=-1, keepdims=True)
    grp = jnp.min(jnp.where(lg == lg_max, lane, LANES), axis=-1, keepdims=True)
    w_grp = 1.0 / jnp.sum(jnp.exp(lg - lg_max), axis=-1, keepdims=True)
    e_lane = lane - MOE_GROUPS
    in_grp = (e_lane >= grp * MOE_EPG) & (e_lane < (grp + 1) * MOE_EPG)
    le = jnp.where(in_grp, logit, -big)
    v1 = jnp.max(le, axis=-1, keepdims=True)
    i1 = jnp.min(jnp.where(le == v1, lane, LANES), axis=-1, keepdims=True)
    le2 = jnp.where(lane == i1, -big, le)
    v2 = jnp.max(le2, axis=-1, keepdims=True)
    i2 = jnp.min(jnp.where(le2 == v2, lane, LANES), axis=-1, keepdims=True)
    r = jnp.exp(v2 - v1)
    w1 = w_grp / (1.0 + r)
    w2 = w_grp * r / (1.0 + r)
    idx_ref[...] = jnp.where(lane == 0, i1 - MOE_GROUPS, jnp.where(lane == 1, i2 - MOE_GROUPS, 0))
    wt_ref[...] = jnp.where(lane == 0, w1, jnp.where(lane == 1, w2, 0.0))


def _xattn(x1, mem_kv, S, wq, wo, g, b, rw, rw_hi, rb):
    T, D = x1.shape
    tm = ROW_TILE
    row = lambda n: pl.BlockSpec((tm, n), lambda i: (i, 0))
    vec = _const_spec((1, D))
    per_step = S // tm
    return pl.pallas_call(
        _xattn_kernel,
        grid=(T // tm,),
        in_specs=[row(D), pl.BlockSpec((1,) + mem_kv.shape[1:], lambda i: (i // per_step, 0, 0)),
                  _const_spec(wq.shape), _const_spec(wo.shape), vec, vec,
                  _const_spec(rw.shape), _const_spec(rw_hi.shape), _const_spec(rb.shape)],
        out_specs=[row(D), pl.BlockSpec((tm * SUBLANES, LANES), lambda i: (i, 0)), row(LANES), row(LANES)],
        out_shape=[jax.ShapeDtypeStruct((T, D), F32), jax.ShapeDtypeStruct((T * SUBLANES, LANES), jnp.int32),
                   jax.ShapeDtypeStruct((T, LANES), jnp.int32), jax.ShapeDtypeStruct((T, LANES), F32)],
        compiler_params=_params(("parallel",)),
        name="xattn",
    )(x1, mem_kv, wq, wo, g, b, rw, rw_hi, rb)


def _tile(ref, t):
    return ref.at[pl.ds(pl.multiple_of(t * SUBLANES, SUBLANES), SUBLANES), :]


def _gather_copy(x_hbm, a, xbuf, sem, r):
    tok = jnp.maximum(a, 0) >> 1
    return pltpu.make_async_copy(_tile(x_hbm, tok), _tile(xbuf, r), sem)


def _scatter_copy(ybuf, a, y_hbm, sem, r, n_real, slot):
    dst = jnp.where(a >= 0, a, n_real + slot * MOE_ROWS + r)
    return pltpu.make_async_copy(_tile(ybuf, r), _tile(y_hbm, dst), sem)


def _moe_ffn_kernel(blk_e_ref, prev_ref, cur_ref, nxt_ref, x_hbm, wg_ref, wu_ref, wd_ref, y_hbm,
                    xbuf, ybuf, wg_sc, wu_sc, wd_sc, gsem, ssem):
    i = pl.program_id(0)

    @pl.when((i == 0) | (blk_e_ref[i] != blk_e_ref[jnp.maximum(i - 1, 0)]))
    def _():
        wg_sc[...] = wg_ref[0].astype(BF16)
        wu_sc[...] = wu_ref[0].astype(BF16)
        wd_sc[...] = wd_ref[0].astype(BF16)

    last = pl.num_programs(0) - 1
    slot = i % 2
    other = 1 - slot
    R = MOE_ROWS
    n_real = 2 * (x_hbm.shape[0] // SUBLANES)
    rows = range(R)
    gather = lambda ref, s, r: _gather_copy(x_hbm, ref[0, 0, r], xbuf.at[s], gsem.at[s], r)
    scatter = lambda a, s, r: _scatter_copy(ybuf.at[s], a, y_hbm, ssem.at[s], r, n_real, s)
    gather_wait = lambda s: pltpu.make_async_copy(xbuf.at[s], xbuf.at[s], gsem.at[s]).wait()
    scatter_wait = lambda s: pltpu.make_async_copy(ybuf.at[s], ybuf.at[s], ssem.at[s]).wait()

    @pl.when(i == 0)
    def _():
        ybuf[...] = jnp.zeros_like(ybuf)
        for r in rows:
            gather(cur_ref, 0, r).start()
            scatter(-1, 0, r).start()

    gather_wait(slot)
    scatter_wait(slot)
    for r in rows:
        gather(nxt_ref, other, r).start(priority=r % 2)
        scatter(jnp.where(i > 0, prev_ref[0, 0, r], -1), other, r).start(priority=r % 2)

    xb = _unpack_halves(_tiles_to_rows(xbuf.at[slot], R))
    hid = jax.nn.silu(_dot(xb, wg_sc[...])) * _dot(xb, wu_sc[...])
    _rows_to_tiles(ybuf.at[slot], _pack_halves(_dot(hid.astype(BF16), wd_sc[...])))

    @pl.when(i == last)
    def _():
        for r in rows:
            scatter(cur_ref[0, 0, r], slot, r).start()
        gather_wait(other)
        scatter_wait(other)
        scatter_wait(slot)


def _moe_ffn(a_buf, blk_e, x2p, w_gate, w_up, w_down):
    T = x2p.shape[0] // SUBLANES
    D = 2 * SUBLANES * LANES
    R = MOE_ROWS
    n_blk = a_buf.shape[0] // R
    Hd = w_gate.shape[2]
    a3 = a_buf.reshape(n_blk, 1, R)
    idx = lambda f: pl.BlockSpec((1, 1, R), lambda i, be: (f(i), 0, 0), memory_space=pltpu.SMEM)
    grid_spec = pltpu.PrefetchScalarGridSpec(
        num_scalar_prefetch=1,
        grid=(n_blk,),
        in_specs=[idx(lambda i: jnp.maximum(i - 1, 0)), idx(lambda i: i), idx(lambda i: jnp.minimum(i + 1, n_blk - 1)),
                  pl.BlockSpec(memory_space=pl.ANY),
                  pl.BlockSpec((1, D, Hd), lambda i, be: (be[i], 0, 0)),
                  pl.BlockSpec((1, D, Hd), lambda i, be: (be[i], 0, 0)),
                  pl.BlockSpec((1, Hd, D), lambda i, be: (be[i], 0, 0))],
        out_specs=pl.BlockSpec(memory_space=pl.ANY),
        scratch_shapes=[pltpu.VMEM((2, R * SUBLANES, LANES), jnp.int32), pltpu.VMEM((2, R * SUBLANES, LANES), jnp.int32),
                        pltpu.VMEM((D, Hd), BF16), pltpu.VMEM((D, Hd), BF16), pltpu.VMEM((Hd, D), BF16),
                        pltpu.SemaphoreType.DMA((2,)), pltpu.SemaphoreType.DMA((2,))],
    )
    return pl.pallas_call(
        _moe_ffn_kernel,
        grid_spec=grid_spec,
        out_shape=jax.ShapeDtypeStruct(((2 * T + 2 * R) * SUBLANES, LANES), jnp.int32),
        compiler_params=_params(("arbitrary",)),
        name="moe_ffn",
    )(blk_e, a3, a3, a3, x2p, w_gate, w_up, w_down)


def _combine_kernel(y_ref, wt_ref, x_ref, g_ref, b_ref, out_ref):
    rows = x_ref.shape[0]
    wt = wt_ref[...]
    y_slot = lambda k: _unpack_halves(_tiles_to_rows(y_ref, rows, k * SUBLANES, 2 * SUBLANES)).astype(F32)
    moe = wt[:, 0:1] * y_slot(0) + wt[:, 1:2] * y_slot(1)
    out_ref[...] = _ln(ALPHA * x_ref[...] + moe, g_ref[...], b_ref[...])


def _combine(y, wts, x2, g, b):
    T, D = x2.shape
    tm = ROW_TILE
    row = lambda n: pl.BlockSpec((tm, n), lambda i: (i, 0))
    return pl.pallas_call(
        _combine_kernel,
        grid=(T // tm,),
        in_specs=[pl.BlockSpec((2 * tm * SUBLANES, LANES), lambda i: (i, 0)), row(LANES), row(D),
                  _const_spec((1, D)), _const_spec((1, D))],
        out_specs=row(D),
        out_shape=jax.ShapeDtypeStruct((T, D), F32),
        compiler_params=_params(("parallel",)),
        name="moe_combine",
    )(y, wts, x2, g, b)


def _t5_bucket(dist):
    n = np.maximum(dist, 0)
    max_exact = REL_BUCKETS // 2
    nf = np.maximum(n, 1).astype(np.float64)
    large = max_exact + (np.log(nf / max_exact) / math.log(REL_MAX_DIST / max_exact)
                         * (REL_BUCKETS - max_exact)).astype(np.int64)
    large = np.minimum(large, REL_BUCKETS - 1)
    return np.where(n < max_exact, n, large)


def _bias_tiles(rel_table):
    Hk, G, QB, NB = NSA_KV_HEADS, NSA_GROUP, Q_BLOCK, STEP_BLOCKS
    tab = rel_table.reshape(REL_BUCKETS, Hk, G).transpose(1, 2, 0) * LOG2E
    far = tab[:, :, REL_BUCKETS - 1]
    far_hi = far.astype(BF16).astype(F32)
    far_lo = far - far_hi
    rel = tab - far[:, :, None]

    def sub_tile(dist, valid):
        bucket = _t5_bucket(dist)
        if not (valid & (bucket != REL_BUCKETS - 1)).any():
            return jnp.broadcast_to(jnp.asarray(np.where(valid, 0.0, NEG), F32), (Hk, G) + dist.shape)
        onehot = jnp.asarray(bucket[None] == np.arange(REL_BUCKETS)[:, None, None], F32)
        bias = jnp.einsum('hgk,kqn->hgqn', rel, onehot, precision=lax.Precision.HIGHEST)
        return jnp.where(jnp.asarray(valid), bias, NEG)

    def tile(dist_fn, valid_fn, n_keys):
        qr = np.arange(QB)[:, None]
        kk = np.arange(LANES)[None, :]
        blocks = []
        for b in range(NB):
            subs = []
            for c in range(n_keys // LANES):
                dist = dist_fn(b, qr, kk + c * LANES)
                subs.append(sub_tile(dist, valid_fn(dist)))
            blocks.append(jnp.concatenate(subs, axis=-1))
        return jnp.stack(blocks, axis=1).reshape(Hk, NB * G * QB, n_keys).astype(F32)

    wt = tile(lambda b, qr, kr: WIN + QB * b + qr - kr, lambda d: (d >= 0) & (d < WIN), WIN + STEP_Q)
    nt = tile(lambda b, qr, kr: STEP_Q + QB * b + qr - kr, lambda d: d >= 0, 2 * STEP_Q)
    frame_lo = CMP_FRAME - LANES
    ct = tile(lambda b, qr, r: (QB * (b - NB) + qr + CMP_STRIDE * (CMP_FRAME - frame_lo - r) - (CMP_BLOCK - 1)),
              lambda d: d >= 0, LANES)
    lane = np.arange(LANES)[None, None, None, :]
    qx = (jnp.where(lane == COL_ONE, far_hi[:, :, None, None], 0.0)
          + jnp.where(lane == COL_ONE2, far_lo[:, :, None, None], 0.0)
          + jnp.where(lane == COL_PAD, NEG, 0.0))
    qx = jnp.broadcast_to(qx[:, None], (Hk, NB, G, QB, LANES)).reshape(Hk, NB * G * QB, LANES)
    return wt, nt, ct, qx.astype(F32)


def _static_tables():
    G = NSA_GROUP
    rows = np.arange(SEL_FRAME * SEL_BLOCK)
    eye = (rows[:, None] // SEL_BLOCK == np.arange(SEL_FRAME)[None, :]).astype(np.float32)
    eye = np.concatenate([np.zeros((KEY_PAD, SEL_FRAME), np.float32), eye], axis=0)
    cs = np.arange(CMP_FRAME) * CMP_STRIDE
    ss = np.arange(SEL_FRAME) * SEL_BLOCK
    ov = np.clip(np.minimum(cs[:, None] + CMP_BLOCK, ss[None, :] + SEL_BLOCK)
                 - np.maximum(cs[:, None], ss[None, :]), 0, None) / CMP_BLOCK
    pick = np.zeros((G, G * HEAD_DIM, LANES), np.float32)
    place = np.zeros((G, LANES, G * HEAD_DIM), np.float32)
    for g in range(G):
        for d in range(HEAD_DIM):
            pick[g, g * HEAD_DIM + d, d] = 1.0
            place[g, d, g * HEAD_DIM + d] = 1.0
    return (jnp.asarray(eye, BF16), jnp.asarray(ov.T, BF16), jnp.asarray(pick, BF16), jnp.asarray(place, BF16))


def _dispatch(idx, T):
    R, E = MOE_ROWS, MOE_EXPERTS
    A = 2 * T
    id_bits = max(A - 1, 1).bit_length()
    e_flat = idx[:, 0:2].reshape(A)
    counts = jnp.sum((e_flat[:, None] == jnp.arange(E, dtype=jnp.int32)[None, :]).astype(jnp.int32), axis=0)
    n_pad = (-counts) % R
    real = (e_flat << (id_bits + 1)) | jnp.arange(A, dtype=jnp.int32)
    d_e = jnp.arange(E, dtype=jnp.int32)[:, None]
    d_r = jnp.arange(R, dtype=jnp.int32)[None, :]
    pad = jnp.where(d_r < n_pad[:, None], (d_e << (id_bits + 1)) | (1 << id_bits) | d_r,
                    (E << (id_bits + 1)) | (d_e * R + d_r))
    keys = jnp.sort(jnp.concatenate([real, pad.reshape(E * R)]), stable=False)
    expert = keys >> (id_bits + 1)
    is_pad = ((keys >> id_bits) & 1) == 1
    a_buf = jnp.where(is_pad | (expert >= E), -1, keys & ((1 << id_bits) - 1))
    blk_e = jnp.minimum(expert.reshape(-1, R)[:, 0], E - 1)
    return a_buf, blk_e


def _forward(x, mem, ln_in_g, ln_in_b, w_in, cmp_pe_k, cmp_w1_k, cmp_w2_k, cmp_pe_v, cmp_w1_v, cmp_w2_v, rel_table, conv_dw_w, conv_dw_b, conv_ln_g, conv_ln_b, w_out, ln1_g, ln1_b, mem_ln_g, mem_ln_b, xa_wq, xa_wkv, xa_wo, ln2_g, ln2_b, router_group_w, router_group_b, router_expert_w, router_expert_b, moe_w_gate, moe_w_up, moe_w_down, ln3_g, ln3_b):
    B, S, D = x.shape
    T = B * S
    Hk, G, dh = NSA_KV_HEADS, NSA_GROUP, HEAD_DIM
    assert S % ROW_TILE == 0 and S % STEP_Q == 0 and SEL_TOPK * SEL_BLOCK <= S <= SEL_FRAME * SEL_BLOCK
    vec = lambda v: v.reshape(1, -1).astype(F32)
    x2d = x.reshape(T, D)

    w = w_in[0]
    nq, nkv = NSA_HEADS * dh, Hk * dh
    c_g = nq + 6 * nkv
    c_u = c_g + NSA_HEADS * 3
    cw = (w.shape[1] - c_u) // 2
    wq = w[:, :nq].astype(BF16)
    wkv = w[:, nq:c_g].astype(BF16)
    wg = jnp.pad(w[:, c_g:c_u].reshape(D, Hk, G * 3), ((0, 0), (0, 0), (0, LANES - G * 3)))
    wg = wg.reshape(D, Hk * LANES).astype(BF16)
    wa = w[:, c_u:c_u + cw].astype(BF16)
    wgt = w[:, c_u + cw:].astype(BF16)
    q, kc_in, vc_in, ks, vs, kw, vw, gates, hglu = _in_proj(x2d, S, vec(ln_in_g), vec(ln_in_b),
                                                             wq, wkv, wg, wa, wgt)

    def compressed(t, pe, w1, w2, ones_cols, pad_col):
        ch = t.reshape(B, Hk, S // CMP_STRIDE, CMP_STRIDE * dh)
        half = CMP_STRIDE * dh
        pe8 = jnp.pad(pe.reshape(1, CMP_BLOCK * dh), ((0, 7), (0, 0))).astype(BF16)
        w2p = jnp.pad(w2, ((0, 0), (0, LANES - dh))).astype(BF16)
        lane = np.arange(LANES)
        tail = np.zeros((8, LANES), np.float32)
        tail[0] = np.isin(lane, ones_cols)
        if pad_col is not None:
            tail[1, pad_col] = 1.0
        w1b16 = w1.astype(BF16)
        return _compress(ch, pe8, w1b16[:half], w1b16[half:], w1b16, w2p, jnp.asarray(tail))

    kc = compressed(kc_in, cmp_pe_k[0], cmp_w1_k[0], cmp_w2_k[0], (COL_ONE, COL_ONE2), COL_PAD)
    vc = compressed(vc_in, cmp_pe_v[0], cmp_w1_v[0], cmp_w2_v[0], (), None)

    eye, ovt, pick, place = _static_tables()
    wt, nt, ct, qx = _bias_tiles(rel_table.astype(F32))
    o_nsa = _nsa(q.reshape(B, S, nq), gates.reshape(B, S, Hk * LANES), kc, vc, ks, vs, kw, vw,
                 eye, ovt, wt, nt, ct, qx, pick, place).reshape(T, nq)

    o_conv = _conv(hglu.reshape(B, S, cw), conv_dw_w[0].reshape(CONV_K, cw), vec(conv_dw_b[0]),
                   vec(conv_ln_g[0]), vec(conv_ln_b[0])).reshape(T, cw)

    wo = w_out[0].astype(BF16)
    x1 = _out_proj(o_nsa, o_conv, x2d, vec(ln_in_g), vec(ln_in_b), wo[:nq], wo[nq:], vec(ln1_g[0]), vec(ln1_b[0]))

    mkv = _mem_kv(mem.reshape(-1, D), vec(mem_ln_g[0]), vec(mem_ln_b[0]), xa_wkv[0].astype(BF16))
    mkv = mkv.reshape(B, mem.shape[1], 2 * D)
    rw = jnp.concatenate([router_group_w[0], router_expert_w[0]], axis=1).astype(F32)
    rw = jnp.pad(rw, ((0, 0), (0, LANES - rw.shape[1])))
    rb = jnp.pad(jnp.concatenate([router_group_b[0], router_expert_b[0]]).astype(F32), (0, LANES - MOE_GROUPS - MOE_EXPERTS))
    rw_hi = rw.astype(BF16)
    rw_lo = (rw - rw_hi.astype(F32)).astype(BF16)
    x2, x2p, idx, wts = _xattn(x1, mkv, S, xa_wq[0].astype(BF16), xa_wo[0].astype(BF16), vec(ln2_g[0]), vec(ln2_b[0]),
                          jnp.concatenate([rw_hi, rw_lo], axis=1), rw_hi, rb.reshape(1, LANES))

    a_buf, blk_e = _dispatch(idx, T)
    y = _moe_ffn(a_buf, blk_e, x2p, moe_w_gate[0], moe_w_up[0], moe_w_down[0])
    out = _combine(y, wts, x2, vec(ln3_g[0]), vec(ln3_b[0]))
    stages = dict(q=q, kc=kc, vc=vc, o_nsa=o_nsa, o_conv=o_conv, x1=x1, x2=x2)
    return out.reshape(B, S, D), stages


def kernel(x, mem, ln_in_g, ln_in_b, w_in, cmp_pe_k, cmp_w1_k, cmp_w2_k, cmp_pe_v, cmp_w1_v, cmp_w2_v, rel_table, conv_dw_w, conv_dw_b, conv_ln_g, conv_ln_b, w_out, ln1_g, ln1_b, mem_ln_g, mem_ln_b, xa_wq, xa_wkv, xa_wo, ln2_g, ln2_b, router_group_w, router_group_b, router_expert_w, router_expert_b, moe_w_gate, moe_w_up, moe_w_down, ln3_g, ln3_b):
    out, _ = _forward(x, mem, ln_in_g, ln_in_b, w_in, cmp_pe_k, cmp_w1_k, cmp_w2_k, cmp_pe_v, cmp_w1_v, cmp_w2_v, rel_table, conv_dw_w, conv_dw_b, conv_ln_g, conv_ln_b, w_out, ln1_g, ln1_b, mem_ln_g, mem_ln_b, xa_wq, xa_wkv, xa_wo, ln2_g, ln2_b, router_group_w, router_group_b, router_expert_w, router_expert_b, moe_w_gate, moe_w_up, moe_w_down, ln3_g, ln3_b)
    return out
```
